```python
import jax, jax.numpy as jnp
from jax import lax
import numpy as np

D_MODEL = 1024
BATCH = 2
SEQ = 16384
DEPTH = 1
DEC_BATCH = 16
DEC_SEQ = 16
PAST_LEN = 4096

CHUNK = 64
N_FOX_HEADS = 8
FOX_HEAD_DIM = 64
FOX_WIDTH = N_FOX_HEADS * FOX_HEAD_DIM
GMLP_WIDTH = D_MODEL - FOX_WIDTH
GMLP_GROUPS = 4
GMLP_GROUP_DIM = GMLP_WIDTH // GMLP_GROUPS
GMLP_CHUNK = 128
IN_WIDTH = 3 * FOX_WIDTH + N_FOX_HEADS + 2 * GMLP_WIDTH
N_MEM = 256
N_MEM_HEADS = 4
MEM_HEAD_DIM = D_MODEL // N_MEM_HEADS
D_FF = -(-8 * D_MODEL // (3 * 256)) * 256
Q_BLOCK = 128
ALPHA = (2.0 * DEPTH) ** 0.25
BETA = (8.0 * DEPTH) ** -0.25
FORGET_BIAS_INIT = 3.0
EPS = 1e-5

kernel_name = 'fox_gmlp_memory_streaming_encoder_step'


def layer_norm(x, g, b):
    xf = x.astype(jnp.float32)
    mu = jnp.mean(xf, axis=-1, keepdims=True)
    var = jnp.mean(jnp.square(xf - mu), axis=-1, keepdims=True)
    return ((xf - mu) * lax.rsqrt(var + EPS) * g + b).astype(x.dtype)


def rms_norm(x, g):
    xf = x.astype(jnp.float32)
    return (xf * lax.rsqrt(jnp.mean(jnp.square(xf), axis=-1, keepdims=True) + EPS) * g).astype(x.dtype)


def in_project(x, w_in, b_f):
    z = x @ w_in
    q, k, v, f_logit, zg = jnp.split(
        z, [FOX_WIDTH, 2 * FOX_WIDTH, 3 * FOX_WIDTH, 3 * FOX_WIDTH + N_FOX_HEADS], axis=-1)
    shp = x.shape[:-1] + (N_FOX_HEADS, FOX_HEAD_DIM)
    log_f = jax.nn.log_sigmoid((f_logit + b_f).astype(jnp.float32))
    return q.reshape(shp), k.reshape(shp), v.reshape(shp), log_f, zg


def fox_prompt(q, k, v, log_f):
    B, S, H, Dh = q.shape
    nblk = S // Q_BLOCK
    c = jnp.cumsum(log_f, axis=1)
    c_k = c.transpose(0, 2, 1)
    qb = q.reshape(B, nblk, Q_BLOCK, H, Dh).transpose(1, 0, 2, 3, 4)
    cb = c.reshape(B, nblk, Q_BLOCK, H).transpose(1, 0, 2, 3)
    key_pos = jnp.arange(S)
    scale = Dh ** -0.5

    def block(args):
        i, q_blk, c_blk = args
        s = jnp.einsum('bqhd,bkhd->bhqk', q_blk, k, preferred_element_type=jnp.float32) * scale
        bias = c_blk.transpose(0, 2, 1)[..., :, None] - c_k[:, :, None, :]
        q_pos = i * Q_BLOCK + jnp.arange(Q_BLOCK)
        s = jnp.where(key_pos[None, :] <= q_pos[:, None], s + bias, -jnp.inf)
        p = jax.nn.softmax(s, axis=-1)
        return jnp.einsum('bhqk,bkhd->bqhd', p.astype(v.dtype), v)

    out = lax.map(block, (jnp.arange(nblk), qb, cb))
    return out.transpose(1, 0, 2, 3, 4).reshape(B, S, H * Dh)


def fox_sample(q, k, v, log_f, cache_k, cache_v, cache_lf):
    Bd, T, H, Dh = q.shape
    lf_c = cache_lf.astype(jnp.float32)
    suffix = lax.cumsum(lf_c, axis=1, reverse=True) - lf_c
    c_new = jnp.cumsum(log_f, axis=1).transpose(0, 2, 1)
    bias_past = c_new[..., :, None] + suffix.transpose(0, 2, 1)[..., None, :]
    causal = jnp.tril(jnp.ones((T, T), dtype=bool))
    bias_new = jnp.where(causal, c_new[..., :, None] - c_new[..., None, :], -jnp.inf)
    k_all = jnp.concatenate([cache_k.astype(k.dtype), k], axis=1)
    v_all = jnp.concatenate([cache_v.astype(v.dtype), v], axis=1)
    s = jnp.einsum('bqhd,bkhd->bhqk', q, k_all, preferred_element_type=jnp.float32) * Dh ** -0.5
    s = s + jnp.concatenate([bias_past, bias_new], axis=-1)
    p = jax.nn.softmax(s, axis=-1)
    out = jnp.einsum('bhqk,bkhd->bqhd', p.astype(v_all.dtype), v_all)
    return out.reshape(Bd, T, H * Dh)


def gmlp_mix(zg, ln_g, ln_b, w_s, b_s, n_chunks, chunk_len):
    B = zg.shape[0]
    z = jax.nn.gelu(zg)
    u, v = jnp.split(z, 2, axis=-1)
    v = layer_norm(v, ln_g, ln_b)
    vb = v.reshape(B, n_chunks, chunk_len, GMLP_GROUPS, GMLP_GROUP_DIM)
    pos = jnp.arange(chunk_len)
    mask = (pos[None, :] // CHUNK) <= (pos[:, None] // CHUNK)
    w = jnp.where(mask[None], w_s[:, :chunk_len, :chunk_len], 0)
    mixed = jnp.einsum('gij,bnjgc->bnigc', w, vb) + b_s[:, :chunk_len].T[None, None, :, :, None]
    return u * mixed.reshape(u.shape), v


def merge_groups(att, gm, g_fox_out, g_gmlp_out, w_o):
    h = jnp.concatenate([rms_norm(att, g_fox_out), rms_norm(gm, g_gmlp_out)], axis=-1)
    return h @ w_o


def memory_kv(mem, w_mk, w_mv):
    shp = mem.shape[:-1] + (N_MEM_HEADS, MEM_HEAD_DIM)
    return (mem @ w_mk).reshape(shp), (mem @ w_mv).reshape(shp)


def cross_attend(x, mk, mv, w_mq, w_mo):
    q = (x @ w_mq).reshape(x.shape[:-1] + (N_MEM_HEADS, MEM_HEAD_DIM))
    s = jnp.einsum('bqhd,bkhd->bhqk', q, mk.astype(q.dtype),
                   preferred_element_type=jnp.float32) * MEM_HEAD_DIM ** -0.5
    p = jax.nn.softmax(s, axis=-1)
    o = jnp.einsum('bhqk,bkhd->bqhd', p.astype(q.dtype), mv.astype(q.dtype))
    return o.reshape(x.shape) @ w_mo


def layer_tail(x, mix, mk, mv, ln1_g, ln1_b, w_mq, w_mo, ln2_g, ln2_b,
               w_gate, w_up, w_down, ln3_g, ln3_b):
    x = layer_norm(ALPHA * x + mix, ln1_g, ln1_b)
    x = layer_norm(ALPHA * x + cross_attend(x, mk, mv, w_mq, w_mo), ln2_g, ln2_b)
    ffn = (jax.nn.silu(x @ w_gate) * (x @ w_up)) @ w_down
    return layer_norm(ALPHA * x + ffn, ln3_g, ln3_b)


def setup_inputs(seed: int = 0) -> dict:
    key = jax.random.key(seed)
    ks = jax.random.split(key, 32)
    L = DEPTH

    def nrm(k, shape, scale=1.0):
        return jax.random.normal(k, shape, jnp.float32) * scale

    def gain(k, n):
        return 1.0 + 0.01 * jax.random.normal(k, (L, n), jnp.float32)

    def bias(k, n):
        return 0.01 * jax.random.normal(k, (L, n), jnp.float32)

    return {
        'x_prompt': nrm(ks[0], (BATCH, SEQ, D_MODEL)),
        'x_sample': nrm(ks[1], (DEC_BATCH, DEC_SEQ, D_MODEL)),
        'cache_fox_k': nrm(ks[2], (L, DEC_BATCH, PAST_LEN, N_FOX_HEADS, FOX_HEAD_DIM)),
        'cache_fox_v': nrm(ks[3], (L, DEC_BATCH, PAST_LEN, N_FOX_HEADS, FOX_HEAD_DIM)),
        'cache_fox_logf': jax.nn.log_sigmoid(FORGET_BIAS_INIT + nrm(ks[4], (L, DEC_BATCH, PAST_LEN, N_FOX_HEADS))),
        'cache_mem_k': nrm(ks[5], (L, DEC_BATCH, N_MEM, N_MEM_HEADS, MEM_HEAD_DIM)),
        'cache_mem_v': nrm(ks[6], (L, DEC_BATCH, N_MEM, N_MEM_HEADS, MEM_HEAD_DIM)),
        'mem_prompt': nrm(ks[7], (BATCH, N_MEM, D_MODEL)),
        'w_in': nrm(ks[8], (L, D_MODEL, IN_WIDTH), D_MODEL ** -0.5),
        'b_f': FORGET_BIAS_INIT + nrm(ks[9], (L, N_FOX_HEADS), 0.1),
        'g_fox_out': gain(ks[10], FOX_WIDTH),
        'g_gmlp_out': gain(ks[11], GMLP_WIDTH),
        'sgu_ln_g': gain(ks[12], GMLP_WIDTH),
        'sgu_ln_b': bias(ks[13], GMLP_WIDTH),
        'w_s': nrm(ks[14], (L, GMLP_GROUPS, GMLP_CHUNK, GMLP_CHUNK), 0.5 * GMLP_CHUNK ** -0.5),
        'b_s': 1.0 + nrm(ks[15], (L, GMLP_GROUPS, GMLP_CHUNK), 0.1),
        'w_o': nrm(ks[16], (L, D_MODEL, D_MODEL), BETA * D_MODEL ** -0.5),
        'ln1_g': gain(ks[17], D_MODEL),
        'ln1_b': bias(ks[18], D_MODEL),
        'w_mq': nrm(ks[19], (L, D_MODEL, D_MODEL), D_MODEL ** -0.5),
        'w_mk': nrm(ks[20], (L, D_MODEL, D_MODEL), D_MODEL ** -0.5),
        'w_mv': nrm(ks[21], (L, D_MODEL, D_MODEL), D_MODEL ** -0.5),
        'w_mo': nrm(ks[22], (L, D_MODEL, D_MODEL), BETA * D_MODEL ** -0.5),
        'ln2_g': gain(ks[23], D_MODEL),
        'ln2_b': bias(ks[24], D_MODEL),
        'w_gate': nrm(ks[25], (L, D_MODEL, D_FF), D_MODEL ** -0.5),
        'w_up': nrm(ks[26], (L, D_MODEL, D_FF), D_MODEL ** -0.5),
        'w_down': nrm(ks[27], (L, D_FF, D_MODEL), BETA * D_FF ** -0.5),
        'ln3_g': gain(ks[28], D_MODEL),
        'ln3_b': bias(ks[29], D_MODEL),
    }


def reference(x_prompt, x_sample, cache_fox_k, cache_fox_v, cache_fox_logf, cache_mem_k, cache_mem_v,
              mem_prompt, w_in, b_f, g_fox_out, g_gmlp_out, sgu_ln_g, sgu_ln_b, w_s, b_s, w_o,
              ln1_g, ln1_b, w_mq, w_mk, w_mv, w_mo, ln2_g, ln2_b, w_gate, w_up, w_down, ln3_g, ln3_b):
    yp, ys = x_prompt, x_sample
    T = x_sample.shape[1]
    kp, vp, lfp, mkp, mvp, ksm, vsm, lfs, gvs = [], [], [], [], [], [], [], [], []
    for l in range(DEPTH):
        tail = dict(ln1_g=ln1_g[l], ln1_b=ln1_b[l], w_mq=w_mq[l], w_mo=w_mo[l], ln2_g=ln2_g[l],
                    ln2_b=ln2_b[l], w_gate=w_gate[l], w_up=w_up[l], w_down=w_down[l],
                    ln3_g=ln3_g[l], ln3_b=ln3_b[l])
        q, k, v, lf, zg = in_project(yp, w_in[l], b_f[l])
        att = fox_prompt(q, k, v, lf)
        gm, _ = gmlp_mix(zg, sgu_ln_g[l], sgu_ln_b[l], w_s[l], b_s[l], yp.shape[1] // GMLP_CHUNK, GMLP_CHUNK)
        mk, mv = memory_kv(mem_prompt, w_mk[l], w_mv[l])
        mix = merge_groups(att, gm, g_fox_out[l], g_gmlp_out[l], w_o[l])
        yp = layer_tail(yp, mix, mk, mv, **tail)
        kp.append(k); vp.append(v); lfp.append(lf); mkp.append(mk); mvp.append(mv)
        q, k, v, lf, zg = in_project(ys, w_in[l], b_f[l])
        att = fox_sample(q, k, v, lf, cache_fox_k[l], cache_fox_v[l], cache_fox_logf[l])
        gm, gv = gmlp_mix(zg, sgu_ln_g[l], sgu_ln_b[l], w_s[l], b_s[l], 1, T)
        mix = merge_groups(att, gm, g_fox_out[l], g_gmlp_out[l], w_o[l])
        ys = layer_tail(ys, mix, cache_mem_k[l], cache_mem_v[l], **tail)
        ksm.append(k); vsm.append(v); lfs.append(lf); gvs.append(gv)
    return (yp, ys, jnp.stack(kp), jnp.stack(vp), jnp.stack(lfp), jnp.stack(mkp), jnp.stack(mvp),
            jnp.stack(ksm), jnp.stack(vsm), jnp.stack(lfs), jnp.stack(gvs))
```

```python
import functools

import jax
import jax.numpy as jnp
from jax import lax
from jax.experimental import pallas as pl
from jax.experimental.pallas import tpu as pltpu

F32 = jnp.float32
BF16 = jnp.bfloat16

LANES = 128
D_MODEL = 1024
STREAM_CHUNK = 64
N_FOX_HEADS = 8
FOX_HEAD_DIM = 64
FOX_WIDTH = N_FOX_HEADS * FOX_HEAD_DIM
HEADS_PER_STEP = LANES // FOX_HEAD_DIM
GMLP_WIDTH = D_MODEL - FOX_WIDTH
GMLP_GROUPS = 4
GMLP_GROUP_DIM = GMLP_WIDTH // GMLP_GROUPS
GMLP_CHUNK = 128
N_MEM = 256
N_MEM_HEADS = 4
MEM_HEAD_DIM = D_MODEL // N_MEM_HEADS
EPS = 1e-5
VMEM_LIMIT = 56 * 1024 * 1024

PROMPT_ROW_TILE = 512
FOX_BLOCK = 512
TAIL_ROW_TILE = 512
FFN_ROW_TILE = 256
FFN_COL_CHUNK = 256
SAMPLE_KEY_CHUNK = 512


def _dot(a, b):
    return jnp.dot(a, b, preferred_element_type=F32)


def _dot_nt(a, b):
    return lax.dot_general(a, b, (((1,), (1,)), ((), ())), preferred_element_type=F32)


def _layer_norm(x, g, b):
    mu = jnp.mean(x, axis=-1, keepdims=True)
    xc = x - mu
    var = jnp.mean(xc * xc, axis=-1, keepdims=True)
    return xc * lax.rsqrt(var + EPS) * g + b


def _rms_norm(x, g):
    return x * lax.rsqrt(jnp.mean(x * x, axis=-1, keepdims=True) + EPS) * g


def _lane_cumsum(x, pos, length):
    shift = 1
    while shift < length:
        x = x + jnp.where(pos >= shift, pltpu.roll(x, shift, axis=1), 0.0)
        shift *= 2
    return x


def _div_pow2(x, n):
    assert n & (n - 1) == 0
    return x >> (n.bit_length() - 1)


def _params(*semantics):
    return pltpu.CompilerParams(dimension_semantics=semantics, vmem_limit_bytes=VMEM_LIMIT)


def _in_proj_kernel(*refs, tm, chunk, seg, prompt):
    (x_ref, wq_ref, wk_ref, wv_ref, wf_ref, bf_ref, wg_ref, lng_ref, lnb_ref,
     ws_ref, bs_ref, gout_ref) = refs[:12]
    if prompt:
        q_ref, k_ref, v_ref, lf_ref, ct_ref, hg_ref, kt_ref, vb_ref, carry_ref = refs[12:]
    else:
        q_ref, k_ref, v_ref, lf_ref, ct_ref, hg_ref, gv_ref = refs[12:]

    xb = x_ref[0].astype(BF16)
    zq = _dot(xb, wq_ref[...])
    q_ref[0] = (zq * (FOX_HEAD_DIM ** -0.5)).astype(BF16)
    zk = _dot(xb, wk_ref[...])
    k_ref[0] = zk
    zv = _dot(xb, wv_ref[...])
    v_ref[0] = zv
    if prompt:
        kt_ref[0] = zk.T.astype(BF16)
        vb_ref[0] = zv.astype(BF16)

    zf = _dot(xb, wf_ref[...]) + bf_ref[...]
    lf = jax.nn.log_sigmoid(zf)
    lf_ref[0] = lf[:, :N_FOX_HEADS]
    lft = lf.T[:N_FOX_HEADS, :]
    lane = lax.broadcasted_iota(jnp.int32, (N_FOX_HEADS, tm), 1)
    if prompt:
        c = _lane_cumsum(lft, lane, tm)

        @pl.when(pl.program_id(1) == 0)
        def _():
            carry_ref[...] = jnp.zeros_like(carry_ref)

        c = c + carry_ref[...]
        carry_ref[...] = jnp.broadcast_to(c[:, tm - 1:tm], (N_FOX_HEADS, tm))
    else:
        c = _lane_cumsum(lft, lane & (seg - 1), seg)
    ct_ref[0] = c

    z = jax.nn.gelu(_dot(xb, wg_ref[...]))
    u = z[:, :GMLP_WIDTH]
    vn = _layer_norm(z[:, GMLP_WIDTH:], lng_ref[...], lnb_ref[...])
    if not prompt:
        gv_ref[0] = vn
    vnb = vn.astype(BF16)
    ri = lax.broadcasted_iota(jnp.int32, (chunk, chunk), 0)
    ci = lax.broadcasted_iota(jnp.int32, (chunk, chunk), 1)
    causal = _div_pow2(ci, STREAM_CHUNK) <= _div_pow2(ri, STREAM_CHUNK)
    wm = [jnp.where(causal, ws_ref[g], 0.0).astype(BF16) for g in range(GMLP_GROUPS)]
    for r0 in range(0, tm, chunk):
        mixed = jnp.concatenate(
            [_dot(wm[g], vnb[r0:r0 + chunk, g * GMLP_GROUP_DIM:(g + 1) * GMLP_GROUP_DIM])
             for g in range(GMLP_GROUPS)], axis=1) + bs_ref[...]
        gm = u[r0:r0 + chunk, :] * mixed
        hg_ref[0, r0:r0 + chunk, :] = _rms_norm(gm, gout_ref[...]).astype(BF16)


def _in_proj_mix(x, wq, wk, wv, wf, bfp, wg, lng, lnb, ws, bs_full, gout, *, tm, chunk, seg, prompt):
    G, R, D = x.shape
    grid = (G, R // tm)
    row = lambda w: pl.BlockSpec((1, tm, w), lambda g, j: (g, j, 0))
    full = lambda a: pl.BlockSpec(a.shape, lambda g, j: (0,) * a.ndim)
    in_specs = [row(D)] + [full(a) for a in (wq, wk, wv, wf, bfp, wg, lng, lnb, ws, bs_full, gout)]
    out_shape = [
        jax.ShapeDtypeStruct((G, R, FOX_WIDTH), BF16),
        jax.ShapeDtypeStruct((G, R, FOX_WIDTH), F32),
        jax.ShapeDtypeStruct((G, R, FOX_WIDTH), F32),
        jax.ShapeDtypeStruct((G, R, N_FOX_HEADS), F32),
        jax.ShapeDtypeStruct((G, N_FOX_HEADS, R), F32),
        jax.ShapeDtypeStruct((G, R, GMLP_WIDTH), BF16),
    ]
    out_specs = [row(FOX_WIDTH), row(FOX_WIDTH), row(FOX_WIDTH), row(N_FOX_HEADS),
                 pl.BlockSpec((1, N_FOX_HEADS, tm), lambda g, j: (g, 0, j)), row(GMLP_WIDTH)]
    scratch = []
    if prompt:
        out_shape += [jax.ShapeDtypeStruct((G, FOX_WIDTH, R), BF16),
                      jax.ShapeDtypeStruct((G, R, FOX_WIDTH), BF16)]
        out_specs += [pl.BlockSpec((1, FOX_WIDTH, tm), lambda g, j: (g, 0, j)), row(FOX_WIDTH)]
        scratch = [pltpu.VMEM((N_FOX_HEADS, tm), F32)]
    else:
        out_shape += [jax.ShapeDtypeStruct((G, R, GMLP_WIDTH), F32)]
        out_specs += [row(GMLP_WIDTH)]
    return pl.pallas_call(
        functools.partial(_in_proj_kernel, tm=tm, chunk=chunk, seg=seg, prompt=prompt),
        grid=grid, in_specs=in_specs, out_specs=out_specs, out_shape=out_shape,
        scratch_shapes=scratch, compiler_params=_params("arbitrary", "arbitrary"),
        name="in_proj_mix_prompt" if prompt else "in_proj_mix_sample",
    )(x, wq, wk, wv, wf, bfp, wg, lng, lnb, ws, bs_full, gout)


def _fox_prompt_kernel(q_ref, kt_ref, v_ref, c_ref, o_ref, m_ref, l_ref, acc_ref, *, tb):
    qi = pl.program_id(2)
    lane = lax.broadcasted_iota(jnp.int32, (tb, LANES), 1)
    row = lax.broadcasted_iota(jnp.int32, (tb, tb), 0)
    col = lax.broadcasted_iota(jnp.int32, (tb, tb), 1)
    q2 = q_ref[0]

    for h in range(HEADS_PER_STEP):
        in_head = _div_pow2(lane, FOX_HEAD_DIM) == h
        qh = jnp.where(in_head, q2, jnp.zeros_like(q2))
        m_ref[...] = jnp.full_like(m_ref, -jnp.inf)
        l_ref[...] = jnp.zeros_like(l_ref)
        acc_ref[...] = jnp.zeros_like(acc_ref)

        def step(j, diagonal, qh=qh, h=h):
            start = pl.multiple_of(j * tb, tb)
            s = _dot(qh, kt_ref[0, :, pl.ds(start, tb)])
            s = s - c_ref[0, 0, h:h + 1, pl.ds(start, tb)]
            if diagonal:
                s = jnp.where(col <= row, s, -jnp.inf)
            m_prev = m_ref[...]
            m_new = jnp.maximum(m_prev, jnp.max(s, axis=1, keepdims=True))
            alpha = jnp.exp(m_prev - m_new)
            p = jnp.exp(s - jnp.concatenate([m_new] * (tb // LANES), axis=1))
            l_ref[...] = alpha * l_ref[...] + jnp.sum(p, axis=1, keepdims=True)
            acc_ref[...] = alpha * acc_ref[...] + _dot(p.astype(BF16), v_ref[0, pl.ds(start, tb), :])
            m_ref[...] = m_new

        def body(j, carry):
            step(j, False)
            return carry

        lax.fori_loop(0, qi, body, 0)
        step(qi, True)
        out = acc_ref[...] / l_ref[...]
        if h == 0:
            o_ref[0] = out
        else:
            o_ref[0] = jnp.where(in_head, out, o_ref[0])


def _fox_prompt(q, kt, vb, ct, *, tb):
    B, S, _ = q.shape
    n_col = FOX_WIDTH // LANES
    c4 = ct.reshape(B, n_col, HEADS_PER_STEP, S)
    return pl.pallas_call(
        functools.partial(_fox_prompt_kernel, tb=tb),
        grid=(B, n_col, S // tb),
        in_specs=[
            pl.BlockSpec((1, tb, LANES), lambda b, g, i: (b, i, g)),
            pl.BlockSpec((1, LANES, S), lambda b, g, i: (b, g, 0)),
            pl.BlockSpec((1, S, LANES), lambda b, g, i: (b, 0, g)),
            pl.BlockSpec((1, 1, HEADS_PER_STEP, S), lambda b, g, i: (b, g, 0, 0)),
        ],
        out_specs=pl.BlockSpec((1, tb, LANES), lambda b, g, i: (b, i, g)),
        out_shape=jax.ShapeDtypeStruct((B, S, FOX_WIDTH), F32),
        scratch_shapes=[pltpu.VMEM((tb, LANES), F32)] * 3,
        compiler_params=_params("arbitrary", "arbitrary", "arbitrary"),
        name="fox_prompt",
    )(q, kt, vb, c4)


def _fox_sample_kernel(q_ref, kn_ref, vn_ref, cn_ref, ck_ref, cv_ref, lft_ref, o_ref, s_ref, *, P, T):
    rows = N_FOX_HEADS * T
    q = q_ref[0]
    qt = jnp.concatenate([q] * N_FOX_HEADS, axis=0)
    r = lax.broadcasted_iota(jnp.int32, (rows, FOX_WIDTH), 0)
    c = lax.broadcasted_iota(jnp.int32, (rows, FOX_WIDTH), 1)
    wt = jnp.where(_div_pow2(r, T) == _div_pow2(c, FOX_HEAD_DIM), qt, jnp.zeros_like(qt))

    lane = lax.broadcasted_iota(jnp.int32, (N_FOX_HEADS, P), 1)
    cs = _lane_cumsum(lft_ref[0], lane, P)
    suffix = cs[:, P - 1:P] - cs

    def head_rows(x):
        return jnp.concatenate(
            [jnp.broadcast_to(x[h:h + 1, :], (T, x.shape[1])) for h in range(N_FOX_HEADS)], axis=0)

    kc_w = SAMPLE_KEY_CHUNK
    for k0 in range(0, P, kc_w):
        kc = ck_ref[0, k0:k0 + kc_w, :].astype(BF16)
        s_ref[:, k0:k0 + kc_w] = _dot_nt(wt, kc) + head_rows(suffix[:, k0:k0 + kc_w])
    pad = jnp.zeros((LANES - T, FOX_WIDTH), BF16)
    sn = _dot_nt(wt, jnp.concatenate([kn_ref[0].astype(BF16), pad], axis=0))
    rr = lax.broadcasted_iota(jnp.int32, (rows, LANES), 0)
    cc = lax.broadcasted_iota(jnp.int32, (rows, LANES), 1)
    s_ref[:, P:P + LANES] = jnp.where(cc <= (rr & (T - 1)), sn - head_rows(cn_ref[0]), -jnp.inf)

    n_cols = P + LANES
    m = jnp.full((rows, 1), -jnp.inf, F32)
    for k0 in range(0, n_cols, kc_w):
        w = min(kc_w, n_cols - k0)
        m = jnp.maximum(m, jnp.max(s_ref[:, k0:k0 + w], axis=1, keepdims=True))
    l = jnp.zeros((rows, 1), F32)
    acc = jnp.zeros((rows, FOX_WIDTH), F32)
    for k0 in range(0, P, kc_w):
        p = jnp.exp(s_ref[:, k0:k0 + kc_w] - m)
        l = l + jnp.sum(p, axis=1, keepdims=True)
        acc = acc + _dot(p.astype(BF16), cv_ref[0, k0:k0 + kc_w, :].astype(BF16))
    p = jnp.exp(s_ref[:, P:P + LANES] - m)
    l = l + jnp.sum(p, axis=1, keepdims=True)
    acc = acc + _dot(p.astype(BF16), jnp.concatenate([vn_ref[0].astype(BF16), pad], axis=0))
    on = acc / l
    c16 = lax.broadcasted_iota(jnp.int32, (T, FOX_WIDTH), 1)
    out = jnp.zeros((T, FOX_WIDTH), F32)
    for h in range(N_FOX_HEADS):
        out = out + jnp.where(_div_pow2(c16, FOX_HEAD_DIM) == h, on[h * T:(h + 1) * T, :], 0.0)
    o_ref[0] = out


def _fox_sample(q, kn, vn, cn_pad, cache_k, cache_v, cache_lft):
    Bd, T, _ = q.shape
    P = cache_k.shape[1]
    per_b = lambda shape: pl.BlockSpec((1,) + shape, lambda b: (b, 0, 0))
    return pl.pallas_call(
        functools.partial(_fox_sample_kernel, P=P, T=T),
        grid=(Bd,),
        in_specs=[per_b((T, FOX_WIDTH)), per_b((T, FOX_WIDTH)), per_b((T, FOX_WIDTH)),
                  per_b((N_FOX_HEADS, LANES)), per_b((P, FOX_WIDTH)), per_b((P, FOX_WIDTH)),
                  per_b((N_FOX_HEADS, P))],
        out_specs=per_b((T, FOX_WIDTH)),
        out_shape=jax.ShapeDtypeStruct((Bd, T, FOX_WIDTH), F32),
        scratch_shapes=[pltpu.VMEM((N_FOX_HEADS * T, P + LANES), F32)],
        compiler_params=_params("arbitrary"),
        name="fox_sample",
    )(q, kn, vn, cn_pad, cache_k, cache_v, cache_lft)


def _memory_kv_kernel(mem_ref, wk_ref, wv_ref, k_ref, v_ref):
    mb = mem_ref[...].astype(BF16)
    k_ref[...] = _dot(mb, wk_ref[...])
    v_ref[...] = _dot(mb, wv_ref[...])


def _memory_kv(mem2d, wk, wv):
    n = mem2d.shape[0]
    full = lambda a: pl.BlockSpec(a.shape, lambda i: (0, 0))
    out = jax.ShapeDtypeStruct((n, D_MODEL), F32)
    return pl.pallas_call(
        _memory_kv_kernel, grid=(1,),
        in_specs=[full(mem2d), full(wk), full(wv)],
        out_specs=[pl.BlockSpec((n, D_MODEL), lambda i: (0, 0))] * 2,
        out_shape=[out, out], compiler_params=_params("arbitrary"), name="memory_kv",
    )(mem2d, wk, wv)


def _merge_xattn_kernel(x_ref, att_ref, hg_ref, mk_ref, mv_ref, gfox_ref, wo_ref, ln1g_ref, ln1b_ref,
                        wmq_ref, wmo_ref, ln2g_ref, ln2b_ref, o_ref, *, alpha):
    an = _rms_norm(att_ref[0], gfox_ref[...]).astype(BF16)
    h = jnp.concatenate([an, hg_ref[0]], axis=1)
    x1 = _layer_norm(alpha * x_ref[0] + _dot(h, wo_ref[...]), ln1g_ref[...], ln1b_ref[...])
    qm = (_dot(x1.astype(BF16), wmq_ref[...]) * (MEM_HEAD_DIM ** -0.5)).astype(BF16)
    mk = mk_ref[0].astype(BF16)
    mv = mv_ref[0].astype(BF16)
    heads = []
    for hh in range(N_MEM_HEADS):
        sl = slice(hh * MEM_HEAD_DIM, (hh + 1) * MEM_HEAD_DIM)
        s = _dot_nt(qm[:, sl], mk[:, sl])
        p = jnp.exp(s - jnp.max(s, axis=1, keepdims=True))
        l = jnp.sum(p, axis=1, keepdims=True)
        heads.append(_dot(p.astype(BF16), mv[:, sl]) / l)
    o = jnp.concatenate(heads, axis=1).astype(BF16)
    o_ref[0] = _layer_norm(alpha * x1 + _dot(o, wmo_ref[...]), ln2g_ref[...], ln2b_ref[...])


def _merge_xattn(x, att, hg, mk, mv, gfox, wo, ln1g, ln1b, wmq, wmo, ln2g, ln2b, *, tm, alpha):
    G, R, D = x.shape
    row = lambda w: pl.BlockSpec((1, tm, w), lambda g, j: (g, j, 0))
    mem = pl.BlockSpec((1, N_MEM, D), lambda g, j: (g, 0, 0))
    full = lambda a: pl.BlockSpec(a.shape, lambda g, j: (0,) * a.ndim)
    return pl.pallas_call(
        functools.partial(_merge_xattn_kernel, alpha=alpha),
        grid=(G, R // tm),
        in_specs=[row(D), row(FOX_WIDTH), row(GMLP_WIDTH), mem, mem] +
                 [full(a) for a in (gfox, wo, ln1g, ln1b, wmq, wmo, ln2g, ln2b)],
        out_specs=row(D), out_shape=jax.ShapeDtypeStruct((G, R, D), F32),
        compiler_params=_params("arbitrary", "arbitrary"), name=f"merge_xattn_{tm}",
    )(x, att, hg, mk, mv, gfox, wo, ln1g, ln1b, wmq, wmo, ln2g, ln2b)


def _ffn_kernel(x_ref, wg_ref, wu_ref, wd_ref, g_ref, b_ref, o_ref, h_ref, *, alpha, d_ff):
    x = x_ref[0]
    xb = x.astype(BF16)
    for c0 in range(0, d_ff, FFN_COL_CHUNK):
        gate = _dot(xb, wg_ref[:, c0:c0 + FFN_COL_CHUNK])
        up = _dot(xb, wu_ref[:, c0:c0 + FFN_COL_CHUNK])
        h_ref[:, c0:c0 + FFN_COL_CHUNK] = (jax.nn.silu(gate) * up).astype(BF16)
    o_ref[0] = _layer_norm(alpha * x + _dot(h_ref[...], wd_ref[...]), g_ref[...], b_ref[...])


def _ffn(x, wg, wu, wd, g, b, *, tm, alpha):
    G, R, D = x.shape
    d_ff = wg.shape[1]
    assert d_ff % FFN_COL_CHUNK == 0
    row = pl.BlockSpec((1, tm, D), lambda gi, j: (gi, j, 0))
    full = lambda a: pl.BlockSpec(a.shape, lambda gi, j: (0,) * a.ndim)
    return pl.pallas_call(
        functools.partial(_ffn_kernel, alpha=alpha, d_ff=d_ff),
        grid=(G, R // tm),
        in_specs=[row] + [full(a) for a in (wg, wu, wd, g, b)],
        out_specs=row, out_shape=jax.ShapeDtypeStruct((G, R, D), F32),
        scratch_shapes=[pltpu.VMEM((tm, d_ff), BF16)],
        compiler_params=_params("arbitrary", "arbitrary"), name=f"ffn_{tm}",
    )(x, wg, wu, wd, g, b)


def _row(a):
    return a.reshape(1, -1)


def kernel(x_prompt, x_sample, cache_fox_k, cache_fox_v, cache_fox_logf, cache_mem_k, cache_mem_v, mem_prompt, w_in, b_f, g_fox_out, g_gmlp_out, sgu_ln_g, sgu_ln_b, w_s, b_s, w_o, ln1_g, ln1_b, w_mq, w_mk, w_mv, w_mo, ln2_g, ln2_b, w_gate, w_up, w_down, ln3_g, ln3_b):
    depth = w_in.shape[0]
    B, S, D = x_prompt.shape
    Bd, T, _ = x_sample.shape
    P = cache_fox_k.shape[2]
    alpha = (2.0 * depth) ** 0.25
    assert D == D_MODEL and S % FOX_BLOCK == 0 and S % PROMPT_ROW_TILE == 0
    assert PROMPT_ROW_TILE % GMLP_CHUNK == 0 and (T & (T - 1)) == 0 and N_FOX_HEADS * T == LANES
    assert P % SAMPLE_KEY_CHUNK == 0 and S % TAIL_ROW_TILE == 0 and S % FFN_ROW_TILE == 0

    yp, ys = x_prompt, x_sample
    outs = [[] for _ in range(9)]
    for l in range(depth):
        w = w_in[l].astype(BF16)
        wq, wk, wv = (w[:, i * FOX_WIDTH:(i + 1) * FOX_WIDTH] for i in range(3))
        f0 = 3 * FOX_WIDTH
        wf = jnp.pad(w[:, f0:f0 + N_FOX_HEADS], ((0, 0), (0, LANES - N_FOX_HEADS)))
        bfp = jnp.pad(_row(b_f[l]), ((0, 0), (0, LANES - N_FOX_HEADS)))
        wg = w[:, f0 + N_FOX_HEADS:]
        lng, lnb, gout = _row(sgu_ln_g[l]), _row(sgu_ln_b[l]), _row(g_gmlp_out[l])
        tail_w = (_row(g_fox_out[l]), w_o[l].astype(BF16), _row(ln1_g[l]), _row(ln1_b[l]),
                  w_mq[l].astype(BF16), w_mo[l].astype(BF16), _row(ln2_g[l]), _row(ln2_b[l]))
        ffn_w = (w_gate[l].astype(BF16), w_up[l].astype(BF16), w_down[l].astype(BF16),
                 _row(ln3_g[l]), _row(ln3_b[l]))

        def mix_params(n):
            bias = jnp.repeat(b_s[l][:, :n].T, GMLP_GROUP_DIM, axis=1)
            return w_s[l][:, :n, :n], bias

        ws_p, bs_p = mix_params(GMLP_CHUNK)
        q, k, v, lf, ct, hg, kt, vb = _in_proj_mix(
            yp, wq, wk, wv, wf, bfp, wg, lng, lnb, ws_p, bs_p, gout,
            tm=PROMPT_ROW_TILE, chunk=GMLP_CHUNK, seg=None, prompt=True)
        att = _fox_prompt(q, kt, vb, ct, tb=FOX_BLOCK)
        mk, mv = _memory_kv(mem_prompt.reshape(B * N_MEM, D), w_mk[l].astype(BF16), w_mv[l].astype(BF16))
        mk, mv = mk.reshape(B, N_MEM, D), mv.reshape(B, N_MEM, D)
        x2 = _merge_xattn(yp, att, hg, mk, mv, *tail_w, tm=TAIL_ROW_TILE, alpha=alpha)
        yp = _ffn(x2, *ffn_w, tm=FFN_ROW_TILE, alpha=alpha)
        outs[0].append(k.reshape(B, S, N_FOX_HEADS, FOX_HEAD_DIM))
        outs[1].append(v.reshape(B, S, N_FOX_HEADS, FOX_HEAD_DIM))
        outs[2].append(lf)
        outs[3].append(mk.reshape(B, N_MEM, N_MEM_HEADS, MEM_HEAD_DIM))
        outs[4].append(mv.reshape(B, N_MEM, N_MEM_HEADS, MEM_HEAD_DIM))

        ws_s, bs_s = mix_params(T)
        q, k, v, lf, ct, hg, gv = _in_proj_mix(
            ys.reshape(1, Bd * T, D), wq, wk, wv, wf, bfp, wg, lng, lnb, ws_s, bs_s, gout,
            tm=Bd * T, chunk=T, seg=T, prompt=False)
        per_b = lambda a: a.reshape(Bd, T, a.shape[-1])
        cn = ct.reshape(N_FOX_HEADS, Bd, T).transpose(1, 0, 2)
        cn = jnp.pad(cn, ((0, 0), (0, 0), (0, LANES - T)))
        att = _fox_sample(per_b(q), per_b(k), per_b(v), cn,
                          cache_fox_k[l].reshape(Bd, P, FOX_WIDTH), cache_fox_v[l].reshape(Bd, P, FOX_WIDTH),
                          jnp.swapaxes(cache_fox_logf[l], 1, 2))
        x2 = _merge_xattn(ys, att, per_b(hg), cache_mem_k[l].reshape(Bd, N_MEM, D),
                          cache_mem_v[l].reshape(Bd, N_MEM, D), *tail_w, tm=T, alpha=alpha)
        ys = _ffn(x2, *ffn_w, tm=T, alpha=alpha)
        outs[5].append(k.reshape(Bd, T, N_FOX_HEADS, FOX_HEAD_DIM))
        outs[6].append(v.reshape(Bd, T, N_FOX_HEADS, FOX_HEAD_DIM))
        outs[7].append(per_b(lf))
        outs[8].append(per_b(gv))
    return (yp, ys) + tuple(jnp.stack(o) for o in outs)
```

```python
import functools

import jax
import jax.numpy as jnp
from jax import lax
from jax.experimental import pallas as pl
from jax.experimental.pallas import tpu as pltpu

F32 = jnp.float32
BF16 = jnp.bfloat16

LANES = 128
D_MODEL = 1024
STREAM_CHUNK = 64
N_FOX_HEADS = 8
FOX_HEAD_DIM = 64
FOX_WIDTH = N_FOX_HEADS * FOX_HEAD_DIM
HEADS_PER_STEP = LANES // FOX_HEAD_DIM
GMLP_WIDTH = D_MODEL - FOX_WIDTH
GMLP_GROUPS = 4
GMLP_GROUP_DIM = GMLP_WIDTH // GMLP_GROUPS
GMLP_CHUNK = 128
N_MEM = 256
N_MEM_HEADS = 4
MEM_HEAD_DIM = D_MODEL // N_MEM_HEADS
EPS = 1e-5
LOG2E = 1.4426950408889634
DECAY_TERMS = 3
DECAY_ROWS = 16
assert HEADS_PER_STEP * DECAY_TERMS <= DECAY_ROWS
VMEM_LIMIT = 56 * 1024 * 1024

PROMPT_ROW_TILE = 512
FOX_BLOCK = 512
TAIL_ROW_TILE = 512
FFN_ROW_TILE = 256
FFN_COL_CHUNK = 256
SAMPLE_KEY_CHUNK = 512


def _dot(a, b):
    return jnp.dot(a, b, preferred_element_type=F32)


def _dot_nt(a, b):
    return lax.dot_general(a, b, (((1,), (1,)), ((), ())), preferred_element_type=F32)


def _layer_norm(x, g, b):
    mu = jnp.mean(x, axis=-1, keepdims=True)
    xc = x - mu
    var = jnp.mean(xc * xc, axis=-1, keepdims=True)
    return xc * lax.rsqrt(var + EPS) * g + b


def _rms_norm(x, g):
    return x * lax.rsqrt(jnp.mean(x * x, axis=-1, keepdims=True) + EPS) * g


def _lane_cumsum(x, pos, length):
    shift = 1
    while shift < length:
        x = x + jnp.where(pos >= shift, pltpu.roll(x, shift, axis=1), 0.0)
        shift *= 2
    return x


def _div_pow2(x, n):
    assert n & (n - 1) == 0
    return x >> (n.bit_length() - 1)


def _params(*semantics):
    return pltpu.CompilerParams(dimension_semantics=semantics, vmem_limit_bytes=VMEM_LIMIT)


def _in_proj_kernel(*refs, tm, chunk, seg, prompt):
    (x_ref, wq_ref, wk_ref, wv_ref, wf_ref, bf_ref, wg_ref, lng_ref, lnb_ref,
     ws_ref, bs_ref, gout_ref) = refs[:12]
    if prompt:
        k_ref, v_ref, lf_ref, hg_ref, qt_ref, kx_ref, vxt_ref, carry_ref = refs[12:]
    else:
        k_ref, v_ref, lf_ref, hg_ref, q_ref, ct_ref, gv_ref = refs[12:]

    xb = x_ref[0].astype(BF16)
    zq = _dot(xb, wq_ref[...])
    if prompt:
        qt_ref[0] = (zq * (FOX_HEAD_DIM ** -0.5 * LOG2E)).T.astype(BF16)
    else:
        q_ref[0] = (zq * (FOX_HEAD_DIM ** -0.5)).astype(BF16)
    zk = _dot(xb, wk_ref[...])
    k_ref[0] = zk
    zv = _dot(xb, wv_ref[...])
    v_ref[0] = zv

    zf = _dot(xb, wf_ref[...]) + bf_ref[...]
    lf = jax.nn.log_sigmoid(zf)
    lf_ref[0] = lf[:, :N_FOX_HEADS]
    lft = lf.T[:N_FOX_HEADS, :]
    lane = lax.broadcasted_iota(jnp.int32, (N_FOX_HEADS, tm), 1)
    if prompt:
        c = _lane_cumsum(lft, lane, tm)

        @pl.when(pl.program_id(1) == 0)
        def _():
            carry_ref[...] = jnp.zeros_like(carry_ref)

        c = c + carry_ref[...]
        carry_ref[...] = jnp.broadcast_to(c[:, tm - 1:tm], (N_FOX_HEADS, tm))

        rest = c * (-LOG2E)
        terms = []
        for _ in range(DECAY_TERMS):
            t = rest.astype(BF16).astype(F32)
            terms.append(t)
            rest = rest - t
        rid = lax.broadcasted_iota(jnp.int32, (DECAY_ROWS, tm), 0)
        for g in range(FOX_WIDTH // LANES):
            dec = jnp.zeros((DECAY_ROWS, tm), F32)
            for hh in range(HEADS_PER_STEP):
                hd = g * HEADS_PER_STEP + hh
                for ti, t in enumerate(terms):
                    dec = jnp.where(rid == hh * DECAY_TERMS + ti, t[hd:hd + 1, :], dec)
            dec = jnp.concatenate([dec, jnp.zeros((LANES - DECAY_ROWS, tm), F32)], axis=0)
            kx_ref[0, :, 2 * g * LANES:(2 * g + 1) * LANES] = zk[:, g * LANES:(g + 1) * LANES].astype(BF16)
            kx_ref[0, :, (2 * g + 1) * LANES:(2 * g + 2) * LANES] = dec.T.astype(BF16)
        zvt = zv.T
        for hd in range(N_FOX_HEADS):
            vxt_ref[0, hd * LANES:hd * LANES + FOX_HEAD_DIM, :] = (
                zvt[hd * FOX_HEAD_DIM:(hd + 1) * FOX_HEAD_DIM, :].astype(BF16))
            vxt_ref[0, hd * LANES + FOX_HEAD_DIM:(hd + 1) * LANES, :] = jnp.ones((LANES - FOX_HEAD_DIM, tm), BF16)
    else:
        ct_ref[0] = _lane_cumsum(lft, lane & (seg - 1), seg)

    z = jax.nn.gelu(_dot(xb, wg_ref[...]))
    u = z[:, :GMLP_WIDTH]
    vn = _layer_norm(z[:, GMLP_WIDTH:], lng_ref[...], lnb_ref[...])
    if not prompt:
        gv_ref[0] = vn
    vnb = vn.astype(BF16)
    ri = lax.broadcasted_iota(jnp.int32, (chunk, chunk), 0)
    ci = lax.broadcasted_iota(jnp.int32, (chunk, chunk), 1)
    causal = _div_pow2(ci, STREAM_CHUNK) <= _div_pow2(ri, STREAM_CHUNK)
    wm = [jnp.where(causal, ws_ref[g], 0.0).astype(BF16) for g in range(GMLP_GROUPS)]
    for r0 in range(0, tm, chunk):
        mixed = jnp.concatenate(
            [_dot(wm[g], vnb[r0:r0 + chunk, g * GMLP_GROUP_DIM:(g + 1) * GMLP_GROUP_DIM])
             for g in range(GMLP_GROUPS)], axis=1) + bs_ref[...]
        gm = u[r0:r0 + chunk, :] * mixed
        hg_ref[0, r0:r0 + chunk, :] = _rms_norm(gm, gout_ref[...]).astype(BF16)


def _in_proj_mix(x, wq, wk, wv, wf, bfp, wg, lng, lnb, ws, bs_full, gout, *, tm, chunk, seg, prompt):
    G, R, D = x.shape
    grid = (G, R // tm)
    row = lambda w: pl.BlockSpec((1, tm, w), lambda g, j: (g, j, 0))
    full = lambda a: pl.BlockSpec(a.shape, lambda g, j: (0,) * a.ndim)
    in_specs = [row(D)] + [full(a) for a in (wq, wk, wv, wf, bfp, wg, lng, lnb, ws, bs_full, gout)]
    col = lambda h: pl.BlockSpec((1, h, tm), lambda g, j: (g, 0, j))
    out_shape = [
        jax.ShapeDtypeStruct((G, R, FOX_WIDTH), F32),
        jax.ShapeDtypeStruct((G, R, FOX_WIDTH), F32),
        jax.ShapeDtypeStruct((G, R, N_FOX_HEADS), F32),
        jax.ShapeDtypeStruct((G, R, GMLP_WIDTH), BF16),
    ]
    out_specs = [row(FOX_WIDTH), row(FOX_WIDTH), row(N_FOX_HEADS), row(GMLP_WIDTH)]
    scratch = []
    if prompt:
        out_shape += [
            jax.ShapeDtypeStruct((G, FOX_WIDTH, R), BF16),
            jax.ShapeDtypeStruct((G, R, 2 * FOX_WIDTH), BF16),
            jax.ShapeDtypeStruct((G, N_FOX_HEADS * LANES, R), BF16)]
        out_specs += [col(FOX_WIDTH), row(2 * FOX_WIDTH), col(N_FOX_HEADS * LANES)]
        scratch = [pltpu.VMEM((N_FOX_HEADS, tm), F32)]
    else:
        out_shape += [jax.ShapeDtypeStruct((G, R, FOX_WIDTH), BF16),
                      jax.ShapeDtypeStruct((G, N_FOX_HEADS, R), F32),
                      jax.ShapeDtypeStruct((G, R, GMLP_WIDTH), F32)]
        out_specs += [row(FOX_WIDTH), col(N_FOX_HEADS), row(GMLP_WIDTH)]
    return pl.pallas_call(
        functools.partial(_in_proj_kernel, tm=tm, chunk=chunk, seg=seg, prompt=prompt),
        grid=grid, in_specs=in_specs, out_specs=out_specs, out_shape=out_shape,
        scratch_shapes=scratch, compiler_params=_params("arbitrary", "arbitrary"),
        name="in_proj_mix_prompt" if prompt else "in_proj_mix_sample",
    )(x, wq, wk, wv, wf, bfp, wg, lng, lnb, ws, bs_full, gout)


def _fox_prompt_kernel(qt_ref, kx_ref, vxt_ref, o_ref, qx_ref, m_ref, acc_ref, s_ref, bm_ref, *, tb):
    qi = pl.program_id(2)
    dim = lax.broadcasted_iota(jnp.int32, (LANES, tb), 0)
    key = lax.broadcasted_iota(jnp.int32, (tb, tb), 0)
    qry = lax.broadcasted_iota(jnp.int32, (tb, tb), 1)
    qt = qt_ref[0]
    for h in range(HEADS_PER_STEP):
        own = _div_pow2(dim, FOX_HEAD_DIM) == h
        decay = (dim >= h * DECAY_TERMS) & (dim < (h + 1) * DECAY_TERMS)
        qx_ref[h, :LANES, :] = jnp.where(own, qt, jnp.zeros_like(qt))
        qx_ref[h, LANES:, :] = jnp.where(decay, 1.0, 0.0).astype(BF16)
    m_ref[...] = jnp.full_like(m_ref, -jnp.inf)
    acc_ref[...] = jnp.zeros_like(acc_ref)

    def scores(j, keep):
        kx = kx_ref[0, pl.ds(pl.multiple_of(j * tb, tb), tb), :]
        out = []
        for h in range(HEADS_PER_STEP):
            s = _dot(kx, qx_ref[h])
            out.append(s if keep is None else jnp.where(keep, s, -jnp.inf))
        return out

    def stash(s_new):
        for h in range(HEADS_PER_STEP):
            s_ref[h] = s_new[h]
            bm_ref[h] = jnp.max(s_new[h], axis=0, keepdims=True)

    def consume(j):
        start = pl.multiple_of(j * tb, tb)
        for h in range(HEADS_PER_STEP):
            m_prev = m_ref[h]
            m_new = jnp.maximum(m_prev, bm_ref[h])
            p = jnp.exp2(s_ref[h] - m_new)
            pv = _dot(vxt_ref[0, h * LANES:(h + 1) * LANES, pl.ds(start, tb)], p.astype(BF16))
            acc_ref[h] = jnp.exp2(m_prev - m_new) * acc_ref[h] + pv
            m_ref[h] = m_new

    def pipelined(j, keep):
        s_next = scores(j + 1, keep)
        consume(j)
        stash(s_next)

    causal = key <= qry
    stash(scores(0, key <= qry + jnp.where(qi > 0, tb, 0)))

    def body(j, carry):
        pipelined(j, None)
        return carry

    lax.fori_loop(0, qi - 1, body, 0)

    @pl.when(qi > 0)
    def _():
        pipelined(qi - 1, causal)

    consume(qi)
    outs = []
    for h in range(HEADS_PER_STEP):
        acc = acc_ref[h]
        outs.append(acc[:FOX_HEAD_DIM, :] / acc[FOX_HEAD_DIM:FOX_HEAD_DIM + 1, :])
    o_ref[0] = jnp.concatenate(outs, axis=0).T


def _fox_prompt(qt, kx, vxt, *, tb):
    B, _, S = qt.shape
    return pl.pallas_call(
        functools.partial(_fox_prompt_kernel, tb=tb),
        grid=(B, FOX_WIDTH // LANES, S // tb),
        in_specs=[
            pl.BlockSpec((1, LANES, tb), lambda b, g, i: (b, g, i)),
            pl.BlockSpec((1, S, 2 * LANES), lambda b, g, i: (b, 0, g)),
            pl.BlockSpec((1, 2 * LANES, S), lambda b, g, i: (b, g, 0)),
        ],
        out_specs=pl.BlockSpec((1, tb, LANES), lambda b, g, i: (b, i, g)),
        out_shape=jax.ShapeDtypeStruct((B, S, FOX_WIDTH), F32),
        scratch_shapes=[pltpu.VMEM((HEADS_PER_STEP, 2 * LANES, tb), BF16),
                        pltpu.VMEM((HEADS_PER_STEP, 1, tb), F32),
                        pltpu.VMEM((HEADS_PER_STEP, LANES, tb), F32),
                        pltpu.VMEM((HEADS_PER_STEP, tb, tb), F32),
                        pltpu.VMEM((HEADS_PER_STEP, 1, tb), F32)],
        compiler_params=_params("arbitrary", "arbitrary", "arbitrary"),
        name="fox_prompt",
    )(qt, kx, vxt)


def _fox_sample_kernel(q_ref, kn_ref, vn_ref, cn_ref, ck_ref, cv_ref, lft_ref, o_ref, s_ref, *, P, T):
    rows = N_FOX_HEADS * T
    q = q_ref[0]
    qt = jnp.concatenate([q] * N_FOX_HEADS, axis=0)
    r = lax.broadcasted_iota(jnp.int32, (rows, FOX_WIDTH), 0)
    c = lax.broadcasted_iota(jnp.int32, (rows, FOX_WIDTH), 1)
    wt = jnp.where(_div_pow2(r, T) == _div_pow2(c, FOX_HEAD_DIM), qt, jnp.zeros_like(qt))

    lane = lax.broadcasted_iota(jnp.int32, (N_FOX_HEADS, P), 1)
    cs = _lane_cumsum(lft_ref[0], lane, P)
    suffix = cs[:, P - 1:P] - cs

    def head_rows(x):
        return jnp.concatenate(
            [jnp.broadcast_to(x[h:h + 1, :], (T, x.shape[1])) for h in range(N_FOX_HEADS)], axis=0)

    kc_w = SAMPLE_KEY_CHUNK
    for k0 in range(0, P, kc_w):
        kc = ck_ref[0, k0:k0 + kc_w, :].astype(BF16)
        s_ref[:, k0:k0 + kc_w] = _dot_nt(wt, kc) + head_rows(suffix[:, k0:k0 + kc_w])
    pad = jnp.zeros((LANES - T, FOX_WIDTH), BF16)
    sn = _dot_nt(wt, jnp.concatenate([kn_ref[0].astype(BF16), pad], axis=0))
    rr = lax.broadcasted_iota(jnp.int32, (rows, LANES), 0)
    cc = lax.broadcasted_iota(jnp.int32, (rows, LANES), 1)
    s_ref[:, P:P + LANES] = jnp.where(cc <= (rr & (T - 1)), sn - head_rows(cn_ref[0]), -jnp.inf)

    n_cols = P + LANES
    m = jnp.full((rows, 1), -jnp.inf, F32)
    for k0 in range(0, n_cols, kc_w):
        w = min(kc_w, n_cols - k0)
        m = jnp.maximum(m, jnp.max(s_ref[:, k0:k0 + w], axis=1, keepdims=True))
    l = jnp.zeros((rows, 1), F32)
    acc = jnp.zeros((rows, FOX_WIDTH), F32)
    for k0 in range(0, P, kc_w):
        p = jnp.exp(s_ref[:, k0:k0 + kc_w] - m)
        l = l + jnp.sum(p, axis=1, keepdims=True)
        acc = acc + _dot(p.astype(BF16), cv_ref[0, k0:k0 + kc_w, :].astype(BF16))
    p = jnp.exp(s_ref[:, P:P + LANES] - m)
    l = l + jnp.sum(p, axis=1, keepdims=True)
    acc = acc + _dot(p.astype(BF16), jnp.concatenate([vn_ref[0].astype(BF16), pad], axis=0))
    on = acc / l
    c16 = lax.broadcasted_iota(jnp.int32, (T, FOX_WIDTH), 1)
    out = jnp.zeros((T, FOX_WIDTH), F32)
    for h in range(N_FOX_HEADS):
        out = out + jnp.where(_div_pow2(c16, FOX_HEAD_DIM) == h, on[h * T:(h + 1) * T, :], 0.0)
    o_ref[0] = out


def _fox_sample(q, kn, vn, cn_pad, cache_k, cache_v, cache_lft):
    Bd, T, _ = q.shape
    P = cache_k.shape[1]
    per_b = lambda shape: pl.BlockSpec((1,) + shape, lambda b: (b, 0, 0))
    return pl.pallas_call(
        functools.partial(_fox_sample_kernel, P=P, T=T),
        grid=(Bd,),
        in_specs=[per_b((T, FOX_WIDTH)), per_b((T, FOX_WIDTH)), per_b((T, FOX_WIDTH)),
                  per_b((N_FOX_HEADS, LANES)), per_b((P, FOX_WIDTH)), per_b((P, FOX_WIDTH)),
                  per_b((N_FOX_HEADS, P))],
        out_specs=per_b((T, FOX_WIDTH)),
        out_shape=jax.ShapeDtypeStruct((Bd, T, FOX_WIDTH), F32),
        scratch_shapes=[pltpu.VMEM((N_FOX_HEADS * T, P + LANES), F32)],
        compiler_params=_params("arbitrary"),
        name="fox_sample",
    )(q, kn, vn, cn_pad, cache_k, cache_v, cache_lft)


def _memory_kv_kernel(mem_ref, wk_ref, wv_ref, k_ref, v_ref):
    mb = mem_ref[...].astype(BF16)
    k_ref[...] = _dot(mb, wk_ref[...])
    v_ref[...] = _dot(mb, wv_ref[...])


def _memory_kv(mem2d, wk, wv):
    n = mem2d.shape[0]
    full = lambda a: pl.BlockSpec(a.shape, lambda i: (0, 0))
    out = jax.ShapeDtypeStruct((n, D_MODEL), F32)
    return pl.pallas_call(
        _memory_kv_kernel, grid=(1,),
        in_specs=[full(mem2d), full(wk), full(wv)],
        out_specs=[pl.BlockSpec((n, D_MODEL), lambda i: (0, 0))] * 2,
        out_shape=[out, out], compiler_params=_params("arbitrary"), name="memory_kv",
    )(mem2d, wk, wv)


def _merge_xattn_kernel(x_ref, att_ref, hg_ref, mk_ref, mv_ref, gfox_ref, wo_ref, ln1g_ref, ln1b_ref,
                        wmq_ref, wmo_ref, ln2g_ref, ln2b_ref, o_ref, *, alpha):
    an = _rms_norm(att_ref[0], gfox_ref[...]).astype(BF16)
    h = jnp.concatenate([an, hg_ref[0]], axis=1)
    x1 = _layer_norm(alpha * x_ref[0] + _dot(h, wo_ref[...]), ln1g_ref[...], ln1b_ref[...])
    qm = (_dot(x1.astype(BF16), wmq_ref[...]) * (MEM_HEAD_DIM ** -0.5)).astype(BF16)
    mk = mk_ref[0].astype(BF16)
    mv = mv_ref[0].astype(BF16)
    heads = []
    for hh in range(N_MEM_HEADS):
        sl = slice(hh * MEM_HEAD_DIM, (hh + 1) * MEM_HEAD_DIM)
        s = _dot_nt(qm[:, sl], mk[:, sl])
        p = jnp.exp(s - jnp.max(s, axis=1, keepdims=True))
        l = jnp.sum(p, axis=1, keepdims=True)
        heads.append(_dot(p.astype(BF16), mv[:, sl]) / l)
    o = jnp.concatenate(heads, axis=1).astype(BF16)
    o_ref[0] = _layer_norm(alpha * x1 + _dot(o, wmo_ref[...]), ln2g_ref[...], ln2b_ref[...])


def _merge_xattn(x, att, hg, mk, mv, gfox, wo, ln1g, ln1b, wmq, wmo, ln2g, ln2b, *, tm, alpha):
    G, R, D = x.shape
    row = lambda w: pl.BlockSpec((1, tm, w), lambda g, j: (g, j, 0))
    mem = pl.BlockSpec((1, N_MEM, D), lambda g, j: (g, 0, 0))
    full = lambda a: pl.BlockSpec(a.shape, lambda g, j: (0,) * a.ndim)
    return pl.pallas_call(
        functools.partial(_merge_xattn_kernel, alpha=alpha),
        grid=(G, R // tm),
        in_specs=[row(D), row(FOX_WIDTH), row(GMLP_WIDTH), mem, mem] +
                 [full(a) for a in (gfox, wo, ln1g, ln1b, wmq, wmo, ln2g, ln2b)],
        out_specs=row(D), out_shape=jax.ShapeDtypeStruct((G, R, D), F32),
        compiler_params=_params("arbitrary", "arbitrary"), name=f"merge_xattn_{tm}",
    )(x, att, hg, mk, mv, gfox, wo, ln1g, ln1b, wmq, wmo, ln2g, ln2b)


def _ffn_kernel(x_ref, wg_ref, wu_ref, wd_ref, g_ref, b_ref, o_ref, h_ref, *, alpha, d_ff):
    x = x_ref[0]
    xb = x.astype(BF16)
    for c0 in range(0, d_ff, FFN_COL_CHUNK):
        gate = _dot(xb, wg_ref[:, c0:c0 + FFN_COL_CHUNK])
        up = _dot(xb, wu_ref[:, c0:c0 + FFN_COL_CHUNK])
        h_ref[:, c0:c0 + FFN_COL_CHUNK] = (jax.nn.silu(gate) * up).astype(BF16)
    o_ref[0] = _layer_norm(alpha * x + _dot(h_ref[...], wd_ref[...]), g_ref[...], b_ref[...])


def _ffn(x, wg, wu, wd, g, b, *, tm, alpha):
    G, R, D = x.shape
    d_ff = wg.shape[1]
    assert d_ff % FFN_COL_CHUNK == 0
    row = pl.BlockSpec((1, tm, D), lambda gi, j: (gi, j, 0))
    full = lambda a: pl.BlockSpec(a.shape, lambda gi, j: (0,) * a.ndim)
    return pl.pallas_call(
        functools.partial(_ffn_kernel, alpha=alpha, d_ff=d_ff),
        grid=(G, R // tm),
        in_specs=[row] + [full(a) for a in (wg, wu, wd, g, b)],
        out_specs=row, out_shape=jax.ShapeDtypeStruct((G, R, D), F32),
        scratch_shapes=[pltpu.VMEM((tm, d_ff), BF16)],
        compiler_params=_params("arbitrary", "arbitrary"), name=f"ffn_{tm}",
    )(x, wg, wu, wd, g, b)


def _row(a):
    return a.reshape(1, -1)


def kernel(x_prompt, x_sample, cache_fox_k, cache_fox_v, cache_fox_logf, cache_mem_k, cache_mem_v, mem_prompt, w_in, b_f, g_fox_out, g_gmlp_out, sgu_ln_g, sgu_ln_b, w_s, b_s, w_o, ln1_g, ln1_b, w_mq, w_mk, w_mv, w_mo, ln2_g, ln2_b, w_gate, w_up, w_down, ln3_g, ln3_b):
    depth = w_in.shape[0]
    B, S, D = x_prompt.shape
    Bd, T, _ = x_sample.shape
    P = cache_fox_k.shape[2]
    alpha = (2.0 * depth) ** 0.25
    assert D == D_MODEL and S % FOX_BLOCK == 0 and S % PROMPT_ROW_TILE == 0
    assert PROMPT_ROW_TILE % GMLP_CHUNK == 0 and (T & (T - 1)) == 0 and N_FOX_HEADS * T == LANES
    assert P % SAMPLE_KEY_CHUNK == 0 and S % TAIL_ROW_TILE == 0 and S % FFN_ROW_TILE == 0

    yp, ys = x_prompt, x_sample
    outs = [[] for _ in range(9)]
    for l in range(depth):
        w = w_in[l].astype(BF16)
        wq, wk, wv = (w[:, i * FOX_WIDTH:(i + 1) * FOX_WIDTH] for i in range(3))
        f0 = 3 * FOX_WIDTH
        wf = jnp.pad(w[:, f0:f0 + N_FOX_HEADS], ((0, 0), (0, LANES - N_FOX_HEADS)))
        bfp = jnp.pad(_row(b_f[l]), ((0, 0), (0, LANES - N_FOX_HEADS)))
        wg = w[:, f0 + N_FOX_HEADS:]
        lng, lnb, gout = _row(sgu_ln_g[l]), _row(sgu_ln_b[l]), _row(g_gmlp_out[l])
        tail_w = (_row(g_fox_out[l]), w_o[l].astype(BF16), _row(ln1_g[l]), _row(ln1_b[l]),
                  w_mq[l].astype(BF16), w_mo[l].astype(BF16), _row(ln2_g[l]), _row(ln2_b[l]))
        ffn_w = (w_gate[l].astype(BF16), w_up[l].astype(BF16), w_down[l].astype(BF16),
                 _row(ln3_g[l]), _row(ln3_b[l]))

        def mix_params(n):
            bias = jnp.repeat(b_s[l][:, :n].T, GMLP_GROUP_DIM, axis=1)
            return w_s[l][:, :n, :n], bias

        ws_p, bs_p = mix_params(GMLP_CHUNK)
        k, v, lf, hg, qt, kx, vxt = _in_proj_mix(
            yp, wq, wk, wv, wf, bfp, wg, lng, lnb, ws_p, bs_p, gout,
            tm=PROMPT_ROW_TILE, chunk=GMLP_CHUNK, seg=None, prompt=True)
        att = _fox_prompt(qt, kx, vxt, tb=FOX_BLOCK)
        mk, mv = _memory_kv(mem_prompt.reshape(B * N_MEM, D), w_mk[l].astype(BF16), w_mv[l].astype(BF16))
        mk, mv = mk.reshape(B, N_MEM, D), mv.reshape(B, N_MEM, D)
        x2 = _merge_xattn(yp, att, hg, mk, mv, *tail_w, tm=TAIL_ROW_TILE, alpha=alpha)
        yp = _ffn(x2, *ffn_w, tm=FFN_ROW_TILE, alpha=alpha)
        outs[0].append(k.reshape(B, S, N_FOX_HEADS, FOX_HEAD_DIM))
        outs[1].append(v.reshape(B, S, N_FOX_HEADS, FOX_HEAD_DIM))
        outs[2].append(lf)
        outs[3].append(mk.reshape(B, N_MEM, N_MEM_HEADS, MEM_HEAD_DIM))
        outs[4].append(mv.reshape(B, N_MEM, N_MEM_HEADS, MEM_HEAD_DIM))

        ws_s, bs_s = mix_params(T)
        k, v, lf, hg, q, ct, gv = _in_proj_mix(
            ys.reshape(1, Bd * T, D), wq, wk, wv, wf, bfp, wg, lng, lnb, ws_s, bs_s, gout,
            tm=Bd * T, chunk=T, seg=T, prompt=False)
        per_b = lambda a: a.reshape(Bd, T, a.shape[-1])
        cn = ct.reshape(N_FOX_HEADS, Bd, T).transpose(1, 0, 2)
        cn = jnp.pad(cn, ((0, 0), (0, 0), (0, LANES - T)))
        att = _fox_sample(per_b(q), per_b(k), per_b(v), cn,
                          cache_fox_k[l].reshape(Bd, P, FOX_WIDTH), cache_fox_v[l].reshape(Bd, P, FOX_WIDTH),
                          jnp.swapaxes(cache_fox_logf[l], 1, 2))
        x2 = _merge_xattn(ys, att, per_b(hg), cache_mem_k[l].reshape(Bd, N_MEM, D),
                          cache_mem_v[l].reshape(Bd, N_MEM, D), *tail_w, tm=T, alpha=alpha)
        ys = _ffn(x2, *ffn_w, tm=T, alpha=alpha)
        outs[5].append(k.reshape(Bd, T, N_FOX_HEADS, FOX_HEAD_DIM))
        outs[6].append(v.reshape(Bd, T, N_FOX_HEADS, FOX_HEAD_DIM))
        outs[7].append(per_b(lf))
        outs[8].append(per_b(gv))
    return (yp, ys) + tuple(jnp.stack(o) for o in outs)
```

```python
import functools

import jax
import jax.numpy as jnp
from jax import lax
from jax.experimental import pallas as pl
from jax.experimental.pallas import tpu as pltpu

F32 = jnp.float32
BF16 = jnp.bfloat16

LANES = 128
D_MODEL = 1024
STREAM_CHUNK = 64
N_FOX_HEADS = 8
FOX_HEAD_DIM = 64
FOX_WIDTH = N_FOX_HEADS * FOX_HEAD_DIM
HEADS_PER_STEP = LANES // FOX_HEAD_DIM
GMLP_WIDTH = D_MODEL - FOX_WIDTH
GMLP_GROUPS = 4
GMLP_GROUP_DIM = GMLP_WIDTH // GMLP_GROUPS
GMLP_CHUNK = 128
N_MEM = 256
N_MEM_HEADS = 4
MEM_HEAD_DIM = D_MODEL // N_MEM_HEADS
EPS = 1e-5
LOG2E = 1.4426950408889634
DECAY_TERMS = 3
DECAY_ROWS = 16
assert HEADS_PER_STEP * DECAY_TERMS <= DECAY_ROWS
VMEM_LIMIT = 56 * 1024 * 1024

PROMPT_ROW_TILE = 512
FOX_BLOCK = 512
KV_UNROLL = 4
TAIL_ROW_TILE = 512
TAIL_PART_MIN_ROWS = 128
FFN_ROW_TILE = 256
FFN_COL_CHUNK = 256
SAMPLE_KEY_CHUNK = 512


def _dot(a, b):
    return jnp.dot(a, b, preferred_element_type=F32)


def _dot_nt(a, b):
    return lax.dot_general(a, b, (((1,), (1,)), ((), ())), preferred_element_type=F32)


def _layer_norm(x, g, b):
    mu = jnp.mean(x, axis=-1, keepdims=True)
    xc = x - mu
    var = jnp.mean(xc * xc, axis=-1, keepdims=True)
    return xc * lax.rsqrt(var + EPS) * g + b


def _rms_norm(x, g):
    return x * lax.rsqrt(jnp.mean(x * x, axis=-1, keepdims=True) + EPS) * g


def _lane_cumsum(x, pos, length):
    shift = 1
    while shift < length:
        x = x + jnp.where(pos >= shift, pltpu.roll(x, shift, axis=1), 0.0)
        shift *= 2
    return x


def _div_pow2(x, n):
    assert n & (n - 1) == 0
    return x >> (n.bit_length() - 1)


def _params(*semantics):
    return pltpu.CompilerParams(dimension_semantics=semantics, vmem_limit_bytes=VMEM_LIMIT)


def _in_proj_kernel(*refs, tm, chunk, seg, prompt):
    (x_ref, wq_ref, wk_ref, wv_ref, wf_ref, bf_ref, wg_ref, lng_ref, lnb_ref,
     ws_ref, bs_ref, gout_ref) = refs[:12]
    if prompt:
        k_ref, v_ref, lf_ref, hg_ref, qt_ref, kx_ref, vxt_ref, carry_ref = refs[12:]
    else:
        k_ref, v_ref, lf_ref, hg_ref, q_ref, ct_ref, gv_ref = refs[12:]

    if prompt:
        @pl.when(pl.program_id(1) == 0)
        def _():
            carry_ref[...] = jnp.zeros_like(carry_ref)

    xb = x_ref[0].astype(BF16)
    zf = _dot(xb, wf_ref[...]) + bf_ref[...]
    zg = _dot(xb, wg_ref[...])
    zq = _dot(xb, wq_ref[...])
    if prompt:
        qt_ref[0] = (zq * (FOX_HEAD_DIM ** -0.5 * LOG2E)).T.astype(BF16)
    else:
        q_ref[0] = (zq * (FOX_HEAD_DIM ** -0.5)).astype(BF16)
    zk = _dot(xb, wk_ref[...])
    k_ref[0] = zk
    zv = _dot(xb, wv_ref[...])
    v_ref[0] = zv

    lf = jax.nn.log_sigmoid(zf)
    lf_ref[0] = lf[:, :N_FOX_HEADS]
    lft = lf.T[:N_FOX_HEADS, :]
    lane = lax.broadcasted_iota(jnp.int32, (N_FOX_HEADS, tm), 1)
    if prompt:
        c = _lane_cumsum(lft, lane, tm) + carry_ref[...]
        carry_ref[...] = jnp.broadcast_to(c[:, tm - 1:tm], (N_FOX_HEADS, tm))

        rest = c * (-LOG2E)
        terms = []
        for _ in range(DECAY_TERMS):
            t = rest.astype(BF16).astype(F32)
            terms.append(t)
            rest = rest - t
        rid = lax.broadcasted_iota(jnp.int32, (DECAY_ROWS, tm), 0)
        for g in range(FOX_WIDTH // LANES):
            dec = jnp.zeros((DECAY_ROWS, tm), F32)
            for hh in range(HEADS_PER_STEP):
                hd = g * HEADS_PER_STEP + hh
                for ti, t in enumerate(terms):
                    dec = jnp.where(rid == hh * DECAY_TERMS + ti, t[hd:hd + 1, :], dec)
            dec = jnp.concatenate([dec, jnp.zeros((LANES - DECAY_ROWS, tm), F32)], axis=0)
            kx_ref[0, :, 2 * g * LANES:(2 * g + 1) * LANES] = zk[:, g * LANES:(g + 1) * LANES].astype(BF16)
            kx_ref[0, :, (2 * g + 1) * LANES:(2 * g + 2) * LANES] = dec.T.astype(BF16)
        zvt = zv.T
        for hd in range(N_FOX_HEADS):
            vxt_ref[0, hd * LANES:hd * LANES + FOX_HEAD_DIM, :] = (
                zvt[hd * FOX_HEAD_DIM:(hd + 1) * FOX_HEAD_DIM, :].astype(BF16))
            vxt_ref[0, hd * LANES + FOX_HEAD_DIM:(hd + 1) * LANES, :] = jnp.ones((LANES - FOX_HEAD_DIM, tm), BF16)
    else:
        ct_ref[0] = _lane_cumsum(lft, lane & (seg - 1), seg)

    z = jax.nn.gelu(zg)
    u = z[:, :GMLP_WIDTH]
    vn = _layer_norm(z[:, GMLP_WIDTH:], lng_ref[...], lnb_ref[...])
    if not prompt:
        gv_ref[0] = vn
    vnb = vn.astype(BF16)
    ri = lax.broadcasted_iota(jnp.int32, (chunk, chunk), 0)
    ci = lax.broadcasted_iota(jnp.int32, (chunk, chunk), 1)
    causal = _div_pow2(ci, STREAM_CHUNK) <= _div_pow2(ri, STREAM_CHUNK)
    wm = [jnp.where(causal, ws_ref[g], 0.0).astype(BF16) for g in range(GMLP_GROUPS)]
    for r0 in range(0, tm, chunk):
        mixed = jnp.concatenate(
            [_dot(wm[g], vnb[r0:r0 + chunk, g * GMLP_GROUP_DIM:(g + 1) * GMLP_GROUP_DIM])
             for g in range(GMLP_GROUPS)], axis=1) + bs_ref[...]
        gm = u[r0:r0 + chunk, :] * mixed
        hg_ref[0, r0:r0 + chunk, :] = _rms_norm(gm, gout_ref[...]).astype(BF16)


def _in_proj_mix(x, wq, wk, wv, wf, bfp, wg, lng, lnb, ws, bs_full, gout, *, tm, chunk, seg, prompt):
    G, R, D = x.shape
    grid = (G, R // tm)
    row = lambda w: pl.BlockSpec((1, tm, w), lambda g, j: (g, j, 0))
    full = lambda a: pl.BlockSpec(a.shape, lambda g, j: (0,) * a.ndim)
    in_specs = [row(D)] + [full(a) for a in (wq, wk, wv, wf, bfp, wg, lng, lnb, ws, bs_full, gout)]
    col = lambda h: pl.BlockSpec((1, h, tm), lambda g, j: (g, 0, j))
    out_shape = [
        jax.ShapeDtypeStruct((G, R, FOX_WIDTH), F32),
        jax.ShapeDtypeStruct((G, R, FOX_WIDTH), F32),
        jax.ShapeDtypeStruct((G, R, N_FOX_HEADS), F32),
        jax.ShapeDtypeStruct((G, R, GMLP_WIDTH), BF16),
    ]
    out_specs = [row(FOX_WIDTH), row(FOX_WIDTH), row(N_FOX_HEADS), row(GMLP_WIDTH)]
    scratch = []
    if prompt:
        out_shape += [
            jax.ShapeDtypeStruct((G, FOX_WIDTH, R), BF16),
            jax.ShapeDtypeStruct((G, R, 2 * FOX_WIDTH), BF16),
            jax.ShapeDtypeStruct((G, N_FOX_HEADS * LANES, R), BF16)]
        out_specs += [col(FOX_WIDTH), row(2 * FOX_WIDTH), col(N_FOX_HEADS * LANES)]
        scratch = [pltpu.VMEM((N_FOX_HEADS, tm), F32)]
    else:
        out_shape += [jax.ShapeDtypeStruct((G, R, FOX_WIDTH), BF16),
                      jax.ShapeDtypeStruct((G, N_FOX_HEADS, R), F32),
                      jax.ShapeDtypeStruct((G, R, GMLP_WIDTH), F32)]
        out_specs += [row(FOX_WIDTH), col(N_FOX_HEADS), row(GMLP_WIDTH)]
    return pl.pallas_call(
        functools.partial(_in_proj_kernel, tm=tm, chunk=chunk, seg=seg, prompt=prompt),
        grid=grid, in_specs=in_specs, out_specs=out_specs, out_shape=out_shape,
        scratch_shapes=scratch, compiler_params=_params("arbitrary", "arbitrary"),
        name="in_proj_mix_prompt" if prompt else "in_proj_mix_sample",
    )(x, wq, wk, wv, wf, bfp, wg, lng, lnb, ws, bs_full, gout)


def _fox_prompt_kernel(qt_ref, kx_ref, vxt_ref, o_ref, qx_ref, m_ref, acc_ref, s_ref, bm_ref, *, tb):
    qi = pl.program_id(2)
    dim = lax.broadcasted_iota(jnp.int32, (LANES, tb), 0)
    key = lax.broadcasted_iota(jnp.int32, (tb, tb), 0)
    qry = lax.broadcasted_iota(jnp.int32, (tb, tb), 1)
    qt = qt_ref[0]
    for h in range(HEADS_PER_STEP):
        own = _div_pow2(dim, FOX_HEAD_DIM) == h
        decay = (dim >= h * DECAY_TERMS) & (dim < (h + 1) * DECAY_TERMS)
        qx_ref[h, :LANES, :] = jnp.where(own, qt, jnp.zeros_like(qt))
        qx_ref[h, LANES:, :] = jnp.where(decay, 1.0, 0.0).astype(BF16)
    m_ref[...] = jnp.full_like(m_ref, -jnp.inf)
    acc_ref[...] = jnp.zeros_like(acc_ref)

    def scores(j, keep):
        kx = kx_ref[0, pl.ds(pl.multiple_of(j * tb, tb), tb), :]
        out = []
        for h in range(HEADS_PER_STEP):
            s = _dot(kx, qx_ref[h])
            out.append(s if keep is None else jnp.where(keep, s, -jnp.inf))
        return out

    def stash(s_new):
        for h in range(HEADS_PER_STEP):
            s_ref[h] = s_new[h]
            bm_ref[h] = jnp.max(s_new[h], axis=0, keepdims=True)

    def consume(j):
        start = pl.multiple_of(j * tb, tb)
        for h in range(HEADS_PER_STEP):
            m_prev = m_ref[h]
            m_new = jnp.maximum(m_prev, bm_ref[h])
            p = jnp.exp2(s_ref[h] - m_new)
            pv = _dot(vxt_ref[0, h * LANES:(h + 1) * LANES, pl.ds(start, tb)], p.astype(BF16))
            acc_ref[h] = jnp.exp2(m_prev - m_new) * acc_ref[h] + pv
            m_ref[h] = m_new

    def pipelined(t):
        s_next = scores(t, None)
        consume(jnp.where(t == 0, qi, t - 1))
        stash(s_next)

    stash(scores(qi, key <= qry))

    def unrolled(t, carry):
        for u in range(KV_UNROLL):
            pipelined(KV_UNROLL * t + u)
        return carry

    def single(t, carry):
        pipelined(t)
        return carry

    n_unrolled = _div_pow2(qi, KV_UNROLL)
    lax.fori_loop(0, n_unrolled, unrolled, 0)
    lax.fori_loop(n_unrolled * KV_UNROLL, qi, single, 0)
    consume(jnp.maximum(qi - 1, 0))
    outs = []
    for h in range(HEADS_PER_STEP):
        acc = acc_ref[h]
        outs.append(acc[:FOX_HEAD_DIM, :] / acc[FOX_HEAD_DIM:FOX_HEAD_DIM + 1, :])
    o_ref[0] = jnp.concatenate(outs, axis=0).T


def _fox_prompt(qt, kx, vxt, *, tb):
    B, _, S = qt.shape
    return pl.pallas_call(
        functools.partial(_fox_prompt_kernel, tb=tb),
        grid=(B, FOX_WIDTH // LANES, S // tb),
        in_specs=[
            pl.BlockSpec((1, LANES, tb), lambda b, g, i: (b, g, i)),
            pl.BlockSpec((1, S, 2 * LANES), lambda b, g, i: (b, 0, g)),
            pl.BlockSpec((1, 2 * LANES, S), lambda b, g, i: (b, g, 0)),
        ],
        out_specs=pl.BlockSpec((1, tb, LANES), lambda b, g, i: (b, i, g)),
        out_shape=jax.ShapeDtypeStruct((B, S, FOX_WIDTH), F32),
        scratch_shapes=[pltpu.VMEM((HEADS_PER_STEP, 2 * LANES, tb), BF16),
                        pltpu.VMEM((HEADS_PER_STEP, 1, tb), F32),
                        pltpu.VMEM((HEADS_PER_STEP, LANES, tb), F32),
                        pltpu.VMEM((HEADS_PER_STEP, tb, tb), F32),
                        pltpu.VMEM((HEADS_PER_STEP, 1, tb), F32)],
        compiler_params=_params("arbitrary", "arbitrary", "arbitrary"),
        name="fox_prompt",
    )(qt, kx, vxt)


def _fox_sample_kernel(q_ref, kn_ref, vn_ref, cn_ref, ck_ref, cv_ref, lft_ref, o_ref, s_ref, *, P, T):
    rows = N_FOX_HEADS * T
    q = q_ref[0]
    qt = jnp.concatenate([q] * N_FOX_HEADS, axis=0)
    r = lax.broadcasted_iota(jnp.int32, (rows, FOX_WIDTH), 0)
    c = lax.broadcasted_iota(jnp.int32, (rows, FOX_WIDTH), 1)
    wt = jnp.where(_div_pow2(r, T) == _div_pow2(c, FOX_HEAD_DIM), qt, jnp.zeros_like(qt))

    lane = lax.broadcasted_iota(jnp.int32, (N_FOX_HEADS, P), 1)
    cs = _lane_cumsum(lft_ref[0], lane, P)
    suffix = cs[:, P - 1:P] - cs

    def head_rows(x):
        return jnp.concatenate(
            [jnp.broadcast_to(x[h:h + 1, :], (T, x.shape[1])) for h in range(N_FOX_HEADS)], axis=0)

    kc_w = SAMPLE_KEY_CHUNK
    for k0 in range(0, P, kc_w):
        kc = ck_ref[0, k0:k0 + kc_w, :].astype(BF16)
        s_ref[:, k0:k0 + kc_w] = _dot_nt(wt, kc) + head_rows(suffix[:, k0:k0 + kc_w])
    pad = jnp.zeros((LANES - T, FOX_WIDTH), BF16)
    sn = _dot_nt(wt, jnp.concatenate([kn_ref[0].astype(BF16), pad], axis=0))
    rr = lax.broadcasted_iota(jnp.int32, (rows, LANES), 0)
    cc = lax.broadcasted_iota(jnp.int32, (rows, LANES), 1)
    s_ref[:, P:P + LANES] = jnp.where(cc <= (rr & (T - 1)), sn - head_rows(cn_ref[0]), -jnp.inf)

    n_cols = P + LANES
    m = jnp.full((rows, 1), -jnp.inf, F32)
    for k0 in range(0, n_cols, kc_w):
        w = min(kc_w, n_cols - k0)
        m = jnp.maximum(m, jnp.max(s_ref[:, k0:k0 + w], axis=1, keepdims=True))
    l = jnp.zeros((rows, 1), F32)
    acc = jnp.zeros((rows, FOX_WIDTH), F32)
    for k0 in range(0, P, kc_w):
        p = jnp.exp(s_ref[:, k0:k0 + kc_w] - m)
        l = l + jnp.sum(p, axis=1, keepdims=True)
        acc = acc + _dot(p.astype(BF16), cv_ref[0, k0:k0 + kc_w, :].astype(BF16))
    p = jnp.exp(s_ref[:, P:P + LANES] - m)
    l = l + jnp.sum(p, axis=1, keepdims=True)
    acc = acc + _dot(p.astype(BF16), jnp.concatenate([vn_ref[0].astype(BF16), pad], axis=0))
    on = acc / l
    c16 = lax.broadcasted_iota(jnp.int32, (T, FOX_WIDTH), 1)
    out = jnp.zeros((T, FOX_WIDTH), F32)
    for h in range(N_FOX_HEADS):
        out = out + jnp.where(_div_pow2(c16, FOX_HEAD_DIM) == h, on[h * T:(h + 1) * T, :], 0.0)
    o_ref[0] = out


def _fox_sample(q, kn, vn, cn_pad, cache_k, cache_v, cache_lft):
    Bd, T, _ = q.shape
    P = cache_k.shape[1]
    per_b = lambda shape: pl.BlockSpec((1,) + shape, lambda b: (b, 0, 0))
    return pl.pallas_call(
        functools.partial(_fox_sample_kernel, P=P, T=T),
        grid=(Bd,),
        in_specs=[per_b((T, FOX_WIDTH)), per_b((T, FOX_WIDTH)), per_b((T, FOX_WIDTH)),
                  per_b((N_FOX_HEADS, LANES)), per_b((P, FOX_WIDTH)), per_b((P, FOX_WIDTH)),
                  per_b((N_FOX_HEADS, P))],
        out_specs=per_b((T, FOX_WIDTH)),
        out_shape=jax.ShapeDtypeStruct((Bd, T, FOX_WIDTH), F32),
        scratch_shapes=[pltpu.VMEM((N_FOX_HEADS * T, P + LANES), F32)],
        compiler_params=_params("arbitrary"),
        name="fox_sample",
    )(q, kn, vn, cn_pad, cache_k, cache_v, cache_lft)


def _memory_kv_kernel(mem_ref, wk_ref, wv_ref, k_ref, v_ref):
    mb = mem_ref[...].astype(BF16)
    k_ref[...] = _dot(mb, wk_ref[...])
    v_ref[...] = _dot(mb, wv_ref[...])


def _memory_kv(mem2d, wk, wv):
    n = mem2d.shape[0]
    full = lambda a: pl.BlockSpec(a.shape, lambda i: (0, 0))
    out = jax.ShapeDtypeStruct((n, D_MODEL), F32)
    return pl.pallas_call(
        _memory_kv_kernel, grid=(1,),
        in_specs=[full(mem2d), full(wk), full(wv)],
        out_specs=[pl.BlockSpec((n, D_MODEL), lambda i: (0, 0))] * 2,
        out_shape=[out, out], compiler_params=_params("arbitrary"), name="memory_kv",
    )(mem2d, wk, wv)


def _merge_xattn_kernel(x_ref, att_ref, hg_ref, mk_ref, mv_ref, gfox_ref, wo_ref, ln1g_ref, ln1b_ref,
                        wmq_ref, wmo_ref, ln2g_ref, ln2b_ref, o_ref, *, alpha):
    tm = x_ref.shape[1]
    n_parts = 2 if tm % (2 * TAIL_PART_MIN_ROWS) == 0 else 1
    parts = [slice(i * tm // n_parts, (i + 1) * tm // n_parts) for i in range(n_parts)]
    head = lambda hh: slice(hh * MEM_HEAD_DIM, (hh + 1) * MEM_HEAD_DIM)
    mk = mk_ref[0].astype(BF16)
    mv = mv_ref[0].astype(BF16)
    h = [jnp.concatenate([_rms_norm(att_ref[0, r, :], gfox_ref[...]).astype(BF16), hg_ref[0, r, :]], axis=1)
         for r in parts]
    mix = [_dot(hi, wo_ref[...]) for hi in h]
    x1 = [_layer_norm(alpha * x_ref[0, r, :] + mi, ln1g_ref[...], ln1b_ref[...]) for r, mi in zip(parts, mix)]
    qm = [_dot(xi.astype(BF16), wmq_ref[...]) for xi in x1]
    qm = [(qi * (MEM_HEAD_DIM ** -0.5)).astype(BF16) for qi in qm]
    s = [[_dot_nt(qi[:, head(hh)], mk[:, head(hh)]) for hh in range(N_MEM_HEADS)] for qi in qm]
    p = [[jnp.exp(sh - jnp.max(sh, axis=1, keepdims=True)) for sh in si] for si in s]
    o = [[_dot(ph.astype(BF16), mv[:, head(hh)]) / jnp.sum(ph, axis=1, keepdims=True)
          for hh, ph in enumerate(pi)] for pi in p]
    y = [_dot(jnp.concatenate(oi, axis=1).astype(BF16), wmo_ref[...]) for oi in o]
    for r, xi, yi in zip(parts, x1, y):
        o_ref[0, r, :] = _layer_norm(alpha * xi + yi, ln2g_ref[...], ln2b_ref[...])


def _merge_xattn(x, att, hg, mk, mv, gfox, wo, ln1g, ln1b, wmq, wmo, ln2g, ln2b, *, tm, alpha):
    G, R, D = x.shape
    row = lambda w: pl.BlockSpec((1, tm, w), lambda g, j: (g, j, 0))
    mem = pl.BlockSpec((1, N_MEM, D), lambda g, j: (g, 0, 0))
    full = lambda a: pl.BlockSpec(a.shape, lambda g, j: (0,) * a.ndim)
    return pl.pallas_call(
        functools.partial(_merge_xattn_kernel, alpha=alpha),
        grid=(G, R // tm),
        in_specs=[row(D), row(FOX_WIDTH), row(GMLP_WIDTH), mem, mem] +
                 [full(a) for a in (gfox, wo, ln1g, ln1b, wmq, wmo, ln2g, ln2b)],
        out_specs=row(D), out_shape=jax.ShapeDtypeStruct((G, R, D), F32),
        compiler_params=_params("arbitrary", "arbitrary"), name=f"merge_xattn_{tm}",
    )(x, att, hg, mk, mv, gfox, wo, ln1g, ln1b, wmq, wmo, ln2g, ln2b)


def _ffn_kernel(x_ref, wg_ref, wu_ref, wd_ref, g_ref, b_ref, o_ref, h_ref, *, alpha, d_ff):
    x = x_ref[0]
    xb = x.astype(BF16)
    for c0 in range(0, d_ff, FFN_COL_CHUNK):
        gate = _dot(xb, wg_ref[:, c0:c0 + FFN_COL_CHUNK])
        up = _dot(xb, wu_ref[:, c0:c0 + FFN_COL_CHUNK])
        h_ref[:, c0:c0 + FFN_COL_CHUNK] = (jax.nn.silu(gate) * up).astype(BF16)
    o_ref[0] = _layer_norm(alpha * x + _dot(h_ref[...], wd_ref[...]), g_ref[...], b_ref[...])


def _ffn(x, wg, wu, wd, g, b, *, tm, alpha):
    G, R, D = x.shape
    d_ff = wg.shape[1]
    assert d_ff % FFN_COL_CHUNK == 0
    row = pl.BlockSpec((1, tm, D), lambda gi, j: (gi, j, 0))
    full = lambda a: pl.BlockSpec(a.shape, lambda gi, j: (0,) * a.ndim)
    return pl.pallas_call(
        functools.partial(_ffn_kernel, alpha=alpha, d_ff=d_ff),
        grid=(G, R // tm),
        in_specs=[row] + [full(a) for a in (wg, wu, wd, g, b)],
        out_specs=row, out_shape=jax.ShapeDtypeStruct((G, R, D), F32),
        scratch_shapes=[pltpu.VMEM((tm, d_ff), BF16)],
        compiler_params=_params("arbitrary", "arbitrary"), name=f"ffn_{tm}",
    )(x, wg, wu, wd, g, b)


def _row(a):
    return a.reshape(1, -1)


def kernel(x_prompt, x_sample, cache_fox_k, cache_fox_v, cache_fox_logf, cache_mem_k, cache_mem_v, mem_prompt, w_in, b_f, g_fox_out, g_gmlp_out, sgu_ln_g, sgu_ln_b, w_s, b_s, w_o, ln1_g, ln1_b, w_mq, w_mk, w_mv, w_mo, ln2_g, ln2_b, w_gate, w_up, w_down, ln3_g, ln3_b):
    depth = w_in.shape[0]
    B, S, D = x_prompt.shape
    Bd, T, _ = x_sample.shape
    P = cache_fox_k.shape[2]
    alpha = (2.0 * depth) ** 0.25
    assert D == D_MODEL and S % FOX_BLOCK == 0 and S % PROMPT_ROW_TILE == 0
    assert PROMPT_ROW_TILE % GMLP_CHUNK == 0 and (T & (T - 1)) == 0 and N_FOX_HEADS * T == LANES
    assert P % SAMPLE_KEY_CHUNK == 0 and S % TAIL_ROW_TILE == 0 and S % FFN_ROW_TILE == 0

    yp, ys = x_prompt, x_sample
    outs = [[] for _ in range(9)]
    for l in range(depth):
        w = w_in[l].astype(BF16)
        wq, wk, wv = (w[:, i * FOX_WIDTH:(i + 1) * FOX_WIDTH] for i in range(3))
        f0 = 3 * FOX_WIDTH
        wf = jnp.pad(w[:, f0:f0 + N_FOX_HEADS], ((0, 0), (0, LANES - N_FOX_HEADS)))
        bfp = jnp.pad(_row(b_f[l]), ((0, 0), (0, LANES - N_FOX_HEADS)))
        wg = w[:, f0 + N_FOX_HEADS:]
        lng, lnb, gout = _row(sgu_ln_g[l]), _row(sgu_ln_b[l]), _row(g_gmlp_out[l])
        tail_w = (_row(g_fox_out[l]), w_o[l].astype(BF16), _row(ln1_g[l]), _row(ln1_b[l]),
                  w_mq[l].astype(BF16), w_mo[l].astype(BF16), _row(ln2_g[l]), _row(ln2_b[l]))
        ffn_w = (w_gate[l].astype(BF16), w_up[l].astype(BF16), w_down[l].astype(BF16),
                 _row(ln3_g[l]), _row(ln3_b[l]))

        def mix_params(n):
            bias = jnp.repeat(b_s[l][:, :n].T, GMLP_GROUP_DIM, axis=1)
            return w_s[l][:, :n, :n], bias

        ws_p, bs_p = mix_params(GMLP_CHUNK)
        k, v, lf, hg, qt, kx, vxt = _in_proj_mix(
            yp, wq, wk, wv, wf, bfp, wg, lng, lnb, ws_p, bs_p, gout,
            tm=PROMPT_ROW_TILE, chunk=GMLP_CHUNK, seg=None, prompt=True)
        att = _fox_prompt(qt, kx, vxt, tb=FOX_BLOCK)
        mk, mv = _memory_kv(mem_prompt.reshape(B * N_MEM, D), w_mk[l].astype(BF16), w_mv[l].astype(BF16))
        mk, mv = mk.reshape(B, N_MEM, D), mv.reshape(B, N_MEM, D)
        x2 = _merge_xattn(yp, att, hg, mk, mv, *tail_w, tm=TAIL_ROW_TILE, alpha=alpha)
        yp = _ffn(x2, *ffn_w, tm=FFN_ROW_TILE, alpha=alpha)
        outs[0].append(k.reshape(B, S, N_FOX_HEADS, FOX_HEAD_DIM))
        outs[1].append(v.reshape(B, S, N_FOX_HEADS, FOX_HEAD_DIM))
        outs[2].append(lf)
        outs[3].append(mk.reshape(B, N_MEM, N_MEM_HEADS, MEM_HEAD_DIM))
        outs[4].append(mv.reshape(B, N_MEM, N_MEM_HEADS, MEM_HEAD_DIM))

        ws_s, bs_s = mix_params(T)
        k, v, lf, hg, q, ct, gv = _in_proj_mix(
            ys.reshape(1, Bd * T, D), wq, wk, wv, wf, bfp, wg, lng, lnb, ws_s, bs_s, gout,
            tm=Bd * T, chunk=T, seg=T, prompt=False)
        per_b = lambda a: a.reshape(Bd, T, a.shape[-1])
        cn = ct.reshape(N_FOX_HEADS, Bd, T).transpose(1, 0, 2)
        cn = jnp.pad(cn, ((0, 0), (0, 0), (0, LANES - T)))
        att = _fox_sample(per_b(q), per_b(k), per_b(v), cn,
                          cache_fox_k[l].reshape(Bd, P, FOX_WIDTH), cache_fox_v[l].reshape(Bd, P, FOX_WIDTH),
                          jnp.swapaxes(cache_fox_logf[l], 1, 2))
        x2 = _merge_xattn(ys, att, per_b(hg), cache_mem_k[l].reshape(Bd, N_MEM, D),
                          cache_mem_v[l].reshape(Bd, N_MEM, D), *tail_w, tm=T, alpha=alpha)
        ys = _ffn(x2, *ffn_w, tm=T, alpha=alpha)
        outs[5].append(k.reshape(Bd, T, N_FOX_HEADS, FOX_HEAD_DIM))
        outs[6].append(v.reshape(Bd, T, N_FOX_HEADS, FOX_HEAD_DIM))
        outs[7].append(per_b(lf))
        outs[8].append(per_b(gv))
    return (yp, ys) + tuple(jnp.stack(o) for o in outs)
```

```python
import functools

import jax
import jax.numpy as jnp
from jax import lax
from jax.experimental import pallas as pl
from jax.experimental.pallas import tpu as pltpu

F32 = jnp.float32
BF16 = jnp.bfloat16

LANES = 128
D_MODEL = 1024
STREAM_CHUNK = 64
N_FOX_HEADS = 8
FOX_HEAD_DIM = 64
FOX_WIDTH = N_FOX_HEADS * FOX_HEAD_DIM
HEADS_PER_STEP = LANES // FOX_HEAD_DIM
GMLP_WIDTH = D_MODEL - FOX_WIDTH
GMLP_GROUPS = 4
GMLP_GROUP_DIM = GMLP_WIDTH // GMLP_GROUPS
GMLP_CHUNK = 128
N_MEM = 256
N_MEM_HEADS = 4
MEM_HEAD_DIM = D_MODEL // N_MEM_HEADS
EPS = 1e-5
LOG2E = 1.4426950408889634
DECAY_TERMS = 3
DECAY_ROWS = 16
assert HEADS_PER_STEP * DECAY_TERMS <= DECAY_ROWS
VMEM_LIMIT = 56 * 1024 * 1024

PROMPT_ROW_TILE = 512
FOX_BLOCK = 512
KV_UNROLL = 4
TAIL_ROW_TILE = 512
TAIL_PART_MIN_ROWS = 128
FFN_ROW_TILE = 512
FFN_COL_CHUNK = 512
SAMPLE_KEY_CHUNK = 512


def _dot(a, b):
    return jnp.dot(a, b, preferred_element_type=F32)


def _dot_nt(a, b):
    return lax.dot_general(a, b, (((1,), (1,)), ((), ())), preferred_element_type=F32)


def _layer_norm(x, g, b):
    mu = jnp.mean(x, axis=-1, keepdims=True)
    xc = x - mu
    var = jnp.mean(xc * xc, axis=-1, keepdims=True)
    return xc * lax.rsqrt(var + EPS) * g + b


def _rms_norm(x, g):
    return x * lax.rsqrt(jnp.mean(x * x, axis=-1, keepdims=True) + EPS) * g


def _lane_cumsum(x, pos, length):
    shift = 1
    while shift < length:
        x = x + jnp.where(pos >= shift, pltpu.roll(x, shift, axis=1), 0.0)
        shift *= 2
    return x


def _div_pow2(x, n):
    assert n & (n - 1) == 0
    return x >> (n.bit_length() - 1)


def _params(*semantics):
    return pltpu.CompilerParams(dimension_semantics=semantics, vmem_limit_bytes=VMEM_LIMIT)


def _in_proj_kernel(*refs, tm, chunk, seg, prompt):
    (x_ref, wq_ref, wk_ref, wv_ref, wf_ref, bf_ref, wg_ref, lng_ref, lnb_ref,
     ws_ref, bs_ref, gout_ref) = refs[:12]
    if prompt:
        k_ref, v_ref, lf_ref, hg_ref, qt_ref, kx_ref, vxt_ref, carry_ref = refs[12:]
    else:
        k_ref, v_ref, lf_ref, hg_ref, q_ref, ct_ref, gv_ref = refs[12:]

    if prompt:
        @pl.when(pl.program_id(1) == 0)
        def _():
            carry_ref[...] = jnp.zeros_like(carry_ref)

    xb = x_ref[0].astype(BF16)
    zf = _dot(xb, wf_ref[...]) + bf_ref[...]
    zg = _dot(xb, wg_ref[...])
    zq = _dot(xb, wq_ref[...])
    if prompt:
        qt_ref[0] = (zq * (FOX_HEAD_DIM ** -0.5 * LOG2E)).T.astype(BF16)
    else:
        q_ref[0] = (zq * (FOX_HEAD_DIM ** -0.5)).astype(BF16)
    zk = _dot(xb, wk_ref[...])
    zv = _dot(xb, wv_ref[...])
    lf = jax.nn.log_sigmoid(zf)
    lft = lf.T[:N_FOX_HEADS, :]
    if prompt:
        zvt = zv.T
        k_ref[0] = zk.T
        v_ref[0] = zvt
        lf_ref[0] = lft
    else:
        k_ref[0] = zk
        v_ref[0] = zv
        lf_ref[0] = lf[:, :N_FOX_HEADS]
    lane = lax.broadcasted_iota(jnp.int32, (N_FOX_HEADS, tm), 1)
    if prompt:
        c = _lane_cumsum(lft, lane, tm) + carry_ref[...]
        carry_ref[...] = jnp.broadcast_to(c[:, tm - 1:tm], (N_FOX_HEADS, tm))

        rest = c * (-LOG2E)
        terms = []
        for _ in range(DECAY_TERMS):
            t = rest.astype(BF16).astype(F32)
            terms.append(t)
            rest = rest - t
        rid = lax.broadcasted_iota(jnp.int32, (DECAY_ROWS, tm), 0)
        for g in range(FOX_WIDTH // LANES):
            dec = jnp.zeros((DECAY_ROWS, tm), F32)
            for hh in range(HEADS_PER_STEP):
                hd = g * HEADS_PER_STEP + hh
                for ti, t in enumerate(terms):
                    dec = jnp.where(rid == hh * DECAY_TERMS + ti, t[hd:hd + 1, :], dec)
            dec = jnp.concatenate([dec, jnp.zeros((LANES - DECAY_ROWS, tm), F32)], axis=0)
            kx_ref[0, :, 2 * g * LANES:(2 * g + 1) * LANES] = zk[:, g * LANES:(g + 1) * LANES].astype(BF16)
            kx_ref[0, :, (2 * g + 1) * LANES:(2 * g + 2) * LANES] = dec.T.astype(BF16)
        for hd in range(N_FOX_HEADS):
            vxt_ref[0, hd * LANES:hd * LANES + FOX_HEAD_DIM, :] = (
                zvt[hd * FOX_HEAD_DIM:(hd + 1) * FOX_HEAD_DIM, :].astype(BF16))
            vxt_ref[0, hd * LANES + FOX_HEAD_DIM:(hd + 1) * LANES, :] = jnp.ones((LANES - FOX_HEAD_DIM, tm), BF16)
    else:
        ct_ref[0] = _lane_cumsum(lft, lane & (seg - 1), seg)

    z = jax.nn.gelu(zg)
    u = z[:, :GMLP_WIDTH]
    vn = _layer_norm(z[:, GMLP_WIDTH:], lng_ref[...], lnb_ref[...])
    if not prompt:
        gv_ref[0] = vn
    vnb = vn.astype(BF16)
    ri = lax.broadcasted_iota(jnp.int32, (chunk, chunk), 0)
    ci = lax.broadcasted_iota(jnp.int32, (chunk, chunk), 1)
    causal = _div_pow2(ci, STREAM_CHUNK) <= _div_pow2(ri, STREAM_CHUNK)
    wm = [jnp.where(causal, ws_ref[g], 0.0).astype(BF16) for g in range(GMLP_GROUPS)]
    for r0 in range(0, tm, chunk):
        mixed = jnp.concatenate(
            [_dot(wm[g], vnb[r0:r0 + chunk, g * GMLP_GROUP_DIM:(g + 1) * GMLP_GROUP_DIM])
             for g in range(GMLP_GROUPS)], axis=1) + bs_ref[...]
        gm = u[r0:r0 + chunk, :] * mixed
        hg_ref[0, r0:r0 + chunk, :] = _rms_norm(gm, gout_ref[...]).astype(BF16)


def _in_proj_mix(x, wq, wk, wv, wf, bfp, wg, lng, lnb, ws, bs_full, gout, *, tm, chunk, seg, prompt):
    G, R, D = x.shape
    grid = (G, R // tm)
    row = lambda w: pl.BlockSpec((1, tm, w), lambda g, j: (g, j, 0))
    full = lambda a: pl.BlockSpec(a.shape, lambda g, j: (0,) * a.ndim)
    in_specs = [row(D)] + [full(a) for a in (wq, wk, wv, wf, bfp, wg, lng, lnb, ws, bs_full, gout)]
    col = lambda h: pl.BlockSpec((1, h, tm), lambda g, j: (g, 0, j))
    wide = (lambda w: (G, w, R)) if prompt else (lambda w: (G, R, w))
    spec = col if prompt else row
    out_shape = [
        jax.ShapeDtypeStruct(wide(FOX_WIDTH), F32),
        jax.ShapeDtypeStruct(wide(FOX_WIDTH), F32),
        jax.ShapeDtypeStruct(wide(N_FOX_HEADS), F32),
        jax.ShapeDtypeStruct((G, R, GMLP_WIDTH), BF16),
    ]
    out_specs = [spec(FOX_WIDTH), spec(FOX_WIDTH), spec(N_FOX_HEADS), row(GMLP_WIDTH)]
    scratch = []
    if prompt:
        out_shape += [
            jax.ShapeDtypeStruct((G, FOX_WIDTH, R), BF16),
            jax.ShapeDtypeStruct((G, R, 2 * FOX_WIDTH), BF16),
            jax.ShapeDtypeStruct((G, N_FOX_HEADS * LANES, R), BF16)]
        out_specs += [col(FOX_WIDTH), row(2 * FOX_WIDTH), col(N_FOX_HEADS * LANES)]
        scratch = [pltpu.VMEM((N_FOX_HEADS, tm), F32)]
    else:
        out_shape += [jax.ShapeDtypeStruct((G, R, FOX_WIDTH), BF16),
                      jax.ShapeDtypeStruct((G, N_FOX_HEADS, R), F32),
                      jax.ShapeDtypeStruct((G, R, GMLP_WIDTH), F32)]
        out_specs += [row(FOX_WIDTH), col(N_FOX_HEADS), row(GMLP_WIDTH)]
    return pl.pallas_call(
        functools.partial(_in_proj_kernel, tm=tm, chunk=chunk, seg=seg, prompt=prompt),
        grid=grid, in_specs=in_specs, out_specs=out_specs, out_shape=out_shape,
        scratch_shapes=scratch, compiler_params=_params("arbitrary", "arbitrary"),
        name="in_proj_mix_prompt" if prompt else "in_proj_mix_sample",
    )(x, wq, wk, wv, wf, bfp, wg, lng, lnb, ws, bs_full, gout)


def _fox_prompt_kernel(qt_ref, kx_ref, vxt_ref, o_ref, qx_ref, m_ref, acc_ref, s_ref, bm_ref, *, tb):
    qi = pl.program_id(2)
    dim = lax.broadcasted_iota(jnp.int32, (LANES, tb), 0)
    key = lax.broadcasted_iota(jnp.int32, (tb, tb), 0)
    qry = lax.broadcasted_iota(jnp.int32, (tb, tb), 1)
    qt = qt_ref[0]
    for h in range(HEADS_PER_STEP):
        own = _div_pow2(dim, FOX_HEAD_DIM) == h
        decay = (dim >= h * DECAY_TERMS) & (dim < (h + 1) * DECAY_TERMS)
        qx_ref[h, :LANES, :] = jnp.where(own, qt, jnp.zeros_like(qt))
        qx_ref[h, LANES:, :] = jnp.where(decay, 1.0, 0.0).astype(BF16)
    m_ref[...] = jnp.full_like(m_ref, -jnp.inf)
    acc_ref[...] = jnp.zeros_like(acc_ref)

    def scores(j, keep):
        kx = kx_ref[0, pl.ds(pl.multiple_of(j * tb, tb), tb), :]
        out = []
        for h in range(HEADS_PER_STEP):
            s = _dot(kx, qx_ref[h])
            out.append(s if keep is None else jnp.where(keep, s, -jnp.inf))
        return out

    def stash(s_new):
        for h in range(HEADS_PER_STEP):
            s_ref[h] = s_new[h]
            bm_ref[h] = jnp.max(s_new[h], axis=0, keepdims=True)

    def consume(j):
        start = pl.multiple_of(j * tb, tb)
        for h in range(HEADS_PER_STEP):
            m_prev = m_ref[h]
            m_new = jnp.maximum(m_prev, bm_ref[h])
            p = jnp.exp2(s_ref[h] - m_new)
            pv = _dot(vxt_ref[0, h * LANES:(h + 1) * LANES, pl.ds(start, tb)], p.astype(BF16))
            acc_ref[h] = jnp.exp2(m_prev - m_new) * acc_ref[h] + pv
            m_ref[h] = m_new

    def pipelined(t):
        s_next = scores(t, None)
        consume(jnp.where(t == 0, qi, t - 1))
        stash(s_next)

    stash(scores(qi, key <= qry))

    def unrolled(t, carry):
        for u in range(KV_UNROLL):
            pipelined(KV_UNROLL * t + u)
        return carry

    def single(t, carry):
        pipelined(t)
        return carry

    n_unrolled = _div_pow2(qi, KV_UNROLL)
    lax.fori_loop(0, n_unrolled, unrolled, 0)
    lax.fori_loop(n_unrolled * KV_UNROLL, qi, single, 0)
    consume(jnp.maximum(qi - 1, 0))
    outs = []
    for h in range(HEADS_PER_STEP):
        acc = acc_ref[h]
        outs.append(acc[:FOX_HEAD_DIM, :] / acc[FOX_HEAD_DIM:FOX_HEAD_DIM + 1, :])
    o_ref[0] = jnp.concatenate(outs, axis=0).T


def _fox_prompt(qt, kx, vxt, *, tb):
    B, _, S = qt.shape
    return pl.pallas_call(
        functools.partial(_fox_prompt_kernel, tb=tb),
        grid=(B, FOX_WIDTH // LANES, S // tb),
        in_specs=[
            pl.BlockSpec((1, LANES, tb), lambda b, g, i: (b, g, i)),
            pl.BlockSpec((1, S, 2 * LANES), lambda b, g, i: (b, 0, g)),
            pl.BlockSpec((1, 2 * LANES, S), lambda b, g, i: (b, g, 0)),
        ],
        out_specs=pl.BlockSpec((1, tb, LANES), lambda b, g, i: (b, i, g)),
        out_shape=jax.ShapeDtypeStruct((B, S, FOX_WIDTH), F32),
        scratch_shapes=[pltpu.VMEM((HEADS_PER_STEP, 2 * LANES, tb), BF16),
                        pltpu.VMEM((HEADS_PER_STEP, 1, tb), F32),
                        pltpu.VMEM((HEADS_PER_STEP, LANES, tb), F32),
                        pltpu.VMEM((HEADS_PER_STEP, tb, tb), F32),
                        pltpu.VMEM((HEADS_PER_STEP, 1, tb), F32)],
        compiler_params=_params("arbitrary", "arbitrary", "arbitrary"),
        name="fox_prompt",
    )(qt, kx, vxt)


def _fox_sample_kernel(q_ref, kn_ref, vn_ref, cn_ref, ck_ref, cv_ref, lft_ref, o_ref, s_ref, *, P, T):
    rows = N_FOX_HEADS * T
    q = q_ref[0]
    qt = jnp.concatenate([q] * N_FOX_HEADS, axis=0)
    r = lax.broadcasted_iota(jnp.int32, (rows, FOX_WIDTH), 0)
    c = lax.broadcasted_iota(jnp.int32, (rows, FOX_WIDTH), 1)
    wt = jnp.where(_div_pow2(r, T) == _div_pow2(c, FOX_HEAD_DIM), qt, jnp.zeros_like(qt))

    lane = lax.broadcasted_iota(jnp.int32, (N_FOX_HEADS, P), 1)
    cs = _lane_cumsum(lft_ref[0], lane, P)
    suffix = cs[:, P - 1:P] - cs

    def head_rows(x):
        return jnp.concatenate(
            [jnp.broadcast_to(x[h:h + 1, :], (T, x.shape[1])) for h in range(N_FOX_HEADS)], axis=0)

    kc_w = SAMPLE_KEY_CHUNK
    for k0 in range(0, P, kc_w):
        kct = ck_ref[0, :, k0:k0 + kc_w].astype(BF16)
        s_ref[:, k0:k0 + kc_w] = _dot(wt, kct) + head_rows(suffix[:, k0:k0 + kc_w])
    pad = jnp.zeros((LANES - T, FOX_WIDTH), BF16)
    sn = _dot_nt(wt, jnp.concatenate([kn_ref[0].astype(BF16), pad], axis=0))
    rr = lax.broadcasted_iota(jnp.int32, (rows, LANES), 0)
    cc = lax.broadcasted_iota(jnp.int32, (rows, LANES), 1)
    s_ref[:, P:P + LANES] = jnp.where(cc <= (rr & (T - 1)), sn - head_rows(cn_ref[0]), -jnp.inf)

    n_cols = P + LANES
    m = jnp.full((rows, 1), -jnp.inf, F32)
    for k0 in range(0, n_cols, kc_w):
        w = min(kc_w, n_cols - k0)
        m = jnp.maximum(m, jnp.max(s_ref[:, k0:k0 + w], axis=1, keepdims=True))
    l = jnp.zeros((rows, 1), F32)
    acc = jnp.zeros((rows, FOX_WIDTH), F32)
    for k0 in range(0, P, kc_w):
        p = jnp.exp(s_ref[:, k0:k0 + kc_w] - m)
        l = l + jnp.sum(p, axis=1, keepdims=True)
        acc = acc + _dot_nt(p.astype(BF16), cv_ref[0, :, k0:k0 + kc_w].astype(BF16))
    p = jnp.exp(s_ref[:, P:P + LANES] - m)
    l = l + jnp.sum(p, axis=1, keepdims=True)
    acc = acc + _dot(p.astype(BF16), jnp.concatenate([vn_ref[0].astype(BF16), pad], axis=0))
    on = acc / l
    c16 = lax.broadcasted_iota(jnp.int32, (T, FOX_WIDTH), 1)
    out = jnp.zeros((T, FOX_WIDTH), F32)
    for h in range(N_FOX_HEADS):
        out = out + jnp.where(_div_pow2(c16, FOX_HEAD_DIM) == h, on[h * T:(h + 1) * T, :], 0.0)
    o_ref[0] = out


def _fox_sample(q, kn, vn, cn_pad, cache_k, cache_v, cache_lft):
    Bd, T, _ = q.shape
    P = cache_k.shape[2]
    per_b = lambda shape: pl.BlockSpec((1,) + shape, lambda b: (b, 0, 0))
    return pl.pallas_call(
        functools.partial(_fox_sample_kernel, P=P, T=T),
        grid=(Bd,),
        in_specs=[per_b((T, FOX_WIDTH)), per_b((T, FOX_WIDTH)), per_b((T, FOX_WIDTH)),
                  per_b((N_FOX_HEADS, LANES)), per_b((FOX_WIDTH, P)), per_b((FOX_WIDTH, P)),
                  per_b((N_FOX_HEADS, P))],
        out_specs=per_b((T, FOX_WIDTH)),
        out_shape=jax.ShapeDtypeStruct((Bd, T, FOX_WIDTH), F32),
        scratch_shapes=[pltpu.VMEM((N_FOX_HEADS * T, P + LANES), F32)],
        compiler_params=_params("arbitrary"),
        name="fox_sample",
    )(q, kn, vn, cn_pad, cache_k, cache_v, cache_lft)


def _memory_kv_kernel(mem_ref, wk_ref, wv_ref, k_ref, v_ref):
    mb = mem_ref[...].astype(BF16)
    k_ref[...] = _dot(mb, wk_ref[...])
    v_ref[...] = _dot(mb, wv_ref[...])


def _memory_kv(mem2d, wk, wv):
    n = mem2d.shape[0]
    full = lambda a: pl.BlockSpec(a.shape, lambda i: (0, 0))
    out = jax.ShapeDtypeStruct((n, D_MODEL), F32)
    return pl.pallas_call(
        _memory_kv_kernel, grid=(1,),
        in_specs=[full(mem2d), full(wk), full(wv)],
        out_specs=[pl.BlockSpec((n, D_MODEL), lambda i: (0, 0))] * 2,
        out_shape=[out, out], compiler_params=_params("arbitrary"), name="memory_kv",
    )(mem2d, wk, wv)


def _merge_xattn_kernel(x_ref, att_ref, hg_ref, mk_ref, mv_ref, gfox_ref, wo_ref, ln1g_ref, ln1b_ref,
                        wmq_ref, wmo_ref, ln2g_ref, ln2b_ref, o_ref, *, alpha):
    tm = x_ref.shape[1]
    n_parts = 2 if tm % (2 * TAIL_PART_MIN_ROWS) == 0 else 1
    parts = [slice(i * tm // n_parts, (i + 1) * tm // n_parts) for i in range(n_parts)]
    head = lambda hh: slice(hh * MEM_HEAD_DIM, (hh + 1) * MEM_HEAD_DIM)
    mk = mk_ref[0].astype(BF16)
    mv = mv_ref[0].astype(BF16)
    h = [jnp.concatenate([_rms_norm(att_ref[0, r, :], gfox_ref[...]).astype(BF16), hg_ref[0, r, :]], axis=1)
         for r in parts]
    mix = [_dot(hi, wo_ref[...]) for hi in h]
    x1 = [_layer_norm(alpha * x_ref[0, r, :] + mi, ln1g_ref[...], ln1b_ref[...]) for r, mi in zip(parts, mix)]
    qm = [_dot(xi.astype(BF16), wmq_ref[...]) for xi in x1]
    qm = [(qi * (MEM_HEAD_DIM ** -0.5)).astype(BF16) for qi in qm]
    s = [[_dot_nt(qi[:, head(hh)], mk[:, head(hh)]) for hh in range(N_MEM_HEADS)] for qi in qm]
    p = [[jnp.exp(sh - jnp.max(sh, axis=1, keepdims=True)) for sh in si] for si in s]
    o = [[_dot(ph.astype(BF16), mv[:, head(hh)]) / jnp.sum(ph, axis=1, keepdims=True)
          for hh, ph in enumerate(pi)] for pi in p]
    y = [_dot(jnp.concatenate(oi, axis=1).astype(BF16), wmo_ref[...]) for oi in o]
    for r, xi, yi in zip(parts, x1, y):
        o_ref[0, r, :] = _layer_norm(alpha * xi + yi, ln2g_ref[...], ln2b_ref[...])


def _merge_xattn(x, att, hg, mk, mv, gfox, wo, ln1g, ln1b, wmq, wmo, ln2g, ln2b, *, tm, alpha):
    G, R, D = x.shape
    row = lambda w: pl.BlockSpec((1, tm, w), lambda g, j: (g, j, 0))
    mem = pl.BlockSpec((1, N_MEM, D), lambda g, j: (g, 0, 0))
    full = lambda a: pl.BlockSpec(a.shape, lambda g, j: (0,) * a.ndim)
    return pl.pallas_call(
        functools.partial(_merge_xattn_kernel, alpha=alpha),
        grid=(G, R // tm),
        in_specs=[row(D), row(FOX_WIDTH), row(GMLP_WIDTH), mem, mem] +
                 [full(a) for a in (gfox, wo, ln1g, ln1b, wmq, wmo, ln2g, ln2b)],
        out_specs=row(D), out_shape=jax.ShapeDtypeStruct((G, R, D), F32),
        compiler_params=_params("arbitrary", "arbitrary"), name=f"merge_xattn_{tm}",
    )(x, att, hg, mk, mv, gfox, wo, ln1g, ln1b, wmq, wmo, ln2g, ln2b)


def _ffn_kernel(x_ref, wg_ref, wu_ref, wd_ref, g_ref, b_ref, o_ref, h_ref, *, alpha, d_ff):
    tm = x_ref.shape[1]
    n_parts = 2 if tm % (2 * TAIL_PART_MIN_ROWS) == 0 else 1
    parts = [slice(i * tm // n_parts, (i + 1) * tm // n_parts) for i in range(n_parts)]
    xb = [x_ref[0, r, :].astype(BF16) for r in parts]
    for c0 in range(0, d_ff, FFN_COL_CHUNK):
        cols = slice(c0, min(c0 + FFN_COL_CHUNK, d_ff))
        for r, xi in zip(parts, xb):
            gate = _dot(xi, wg_ref[:, cols])
            up = _dot(xi, wu_ref[:, cols])
            h_ref[r, cols] = (jax.nn.silu(gate) * up).astype(BF16)
    y = [_dot(h_ref[r, :], wd_ref[...]) for r in parts]
    for r, yi in zip(parts, y):
        o_ref[0, r, :] = _layer_norm(alpha * x_ref[0, r, :] + yi, g_ref[...], b_ref[...])


def _ffn(x, wg, wu, wd, g, b, *, tm, alpha):
    G, R, D = x.shape
    d_ff = wg.shape[1]
    assert d_ff % LANES == 0
    row = pl.BlockSpec((1, tm, D), lambda gi, j: (gi, j, 0))
    full = lambda a: pl.BlockSpec(a.shape, lambda gi, j: (0,) * a.ndim)
    return pl.pallas_call(
        functools.partial(_ffn_kernel, alpha=alpha, d_ff=d_ff),
        grid=(G, R // tm),
        in_specs=[row] + [full(a) for a in (wg, wu, wd, g, b)],
        out_specs=row, out_shape=jax.ShapeDtypeStruct((G, R, D), F32),
        scratch_shapes=[pltpu.VMEM((tm, d_ff), BF16)],
        compiler_params=_params("arbitrary", "arbitrary"), name=f"ffn_{tm}",
    )(x, wg, wu, wd, g, b)


def _row(a):
    return a.reshape(1, -1)


def kernel(x_prompt, x_sample, cache_fox_k, cache_fox_v, cache_fox_logf, cache_mem_k, cache_mem_v, mem_prompt, w_in, b_f, g_fox_out, g_gmlp_out, sgu_ln_g, sgu_ln_b, w_s, b_s, w_o, ln1_g, ln1_b, w_mq, w_mk, w_mv, w_mo, ln2_g, ln2_b, w_gate, w_up, w_down, ln3_g, ln3_b):
    depth = w_in.shape[0]
    B, S, D = x_prompt.shape
    Bd, T, _ = x_sample.shape
    P = cache_fox_k.shape[2]
    alpha = (2.0 * depth) ** 0.25
    assert D == D_MODEL and S % FOX_BLOCK == 0 and S % PROMPT_ROW_TILE == 0
    assert PROMPT_ROW_TILE % GMLP_CHUNK == 0 and (T & (T - 1)) == 0 and N_FOX_HEADS * T == LANES
    assert P % SAMPLE_KEY_CHUNK == 0 and S % TAIL_ROW_TILE == 0 and S % FFN_ROW_TILE == 0

    yp, ys = x_prompt, x_sample
    outs = [[] for _ in range(9)]
    for l in range(depth):
        w = w_in[l].astype(BF16)
        wq, wk, wv = (w[:, i * FOX_WIDTH:(i + 1) * FOX_WIDTH] for i in range(3))
        f0 = 3 * FOX_WIDTH
        wf = jnp.pad(w[:, f0:f0 + N_FOX_HEADS], ((0, 0), (0, LANES - N_FOX_HEADS)))
        bfp = jnp.pad(_row(b_f[l]), ((0, 0), (0, LANES - N_FOX_HEADS)))
        wg = w[:, f0 + N_FOX_HEADS:]
        lng, lnb, gout = _row(sgu_ln_g[l]), _row(sgu_ln_b[l]), _row(g_gmlp_out[l])
        tail_w = (_row(g_fox_out[l]), w_o[l].astype(BF16), _row(ln1_g[l]), _row(ln1_b[l]),
                  w_mq[l].astype(BF16), w_mo[l].astype(BF16), _row(ln2_g[l]), _row(ln2_b[l]))
        ffn_w = (w_gate[l].astype(BF16), w_up[l].astype(BF16), w_down[l].astype(BF16),
                 _row(ln3_g[l]), _row(ln3_b[l]))

        def mix_params(n):
            bias = jnp.repeat(b_s[l][:, :n].T, GMLP_GROUP_DIM, axis=1)
            return w_s[l][:, :n, :n], bias

        ws_p, bs_p = mix_params(GMLP_CHUNK)
        k, v, lf, hg, qt, kx, vxt = _in_proj_mix(
            yp, wq, wk, wv, wf, bfp, wg, lng, lnb, ws_p, bs_p, gout,
            tm=PROMPT_ROW_TILE, chunk=GMLP_CHUNK, seg=None, prompt=True)
        att = _fox_prompt(qt, kx, vxt, tb=FOX_BLOCK)
        mk, mv = _memory_kv(mem_prompt.reshape(B * N_MEM, D), w_mk[l].astype(BF16), w_mv[l].astype(BF16))
        mk, mv = mk.reshape(B, N_MEM, D), mv.reshape(B, N_MEM, D)
        x2 = _merge_xattn(yp, att, hg, mk, mv, *tail_w, tm=TAIL_ROW_TILE, alpha=alpha)
        yp = _ffn(x2, *ffn_w, tm=FFN_ROW_TILE, alpha=alpha)
        heads_last = lambda a: a.reshape(B, N_FOX_HEADS, FOX_HEAD_DIM, S).transpose(0, 3, 1, 2)
        outs[0].append(heads_last(k))
        outs[1].append(heads_last(v))
        outs[2].append(lf.transpose(0, 2, 1))
        outs[3].append(mk.reshape(B, N_MEM, N_MEM_HEADS, MEM_HEAD_DIM))
        outs[4].append(mv.reshape(B, N_MEM, N_MEM_HEADS, MEM_HEAD_DIM))

        ws_s, bs_s = mix_params(T)
        k, v, lf, hg, q, ct, gv = _in_proj_mix(
            ys.reshape(1, Bd * T, D), wq, wk, wv, wf, bfp, wg, lng, lnb, ws_s, bs_s, gout,
            tm=Bd * T, chunk=T, seg=T, prompt=False)
        per_b = lambda a: a.reshape(Bd, T, a.shape[-1])
        cn = ct.reshape(N_FOX_HEADS, Bd, T).transpose(1, 0, 2)
        cn = jnp.pad(cn, ((0, 0), (0, 0), (0, LANES - T)))
        dims_first = lambda a: a.transpose(0, 2, 3, 1).reshape(Bd, FOX_WIDTH, P)
        att = _fox_sample(per_b(q), per_b(k), per_b(v), cn,
                          dims_first(cache_fox_k[l]), dims_first(cache_fox_v[l]),
                          jnp.swapaxes(cache_fox_logf[l], 1, 2))
        x2 = _merge_xattn(ys, att, per_b(hg), cache_mem_k[l].reshape(Bd, N_MEM, D),
                          cache_mem_v[l].reshape(Bd, N_MEM, D), *tail_w, tm=T, alpha=alpha)
        ys = _ffn(x2.reshape(1, Bd * T, D), *ffn_w, tm=Bd * T, alpha=alpha).reshape(Bd, T, D)
        outs[5].append(k.reshape(Bd, T, N_FOX_HEADS, FOX_HEAD_DIM))
        outs[6].append(v.reshape(Bd, T, N_FOX_HEADS, FOX_HEAD_DIM))
        outs[7].append(per_b(lf))
        outs[8].append(per_b(gv))
    return (yp, ys) + tuple(jnp.stack(o) for o in outs)
```

```python
import functools

import jax
import jax.numpy as jnp
from jax import lax
from jax.experimental import pallas as pl
from jax.experimental.pallas import tpu as pltpu

F32 = jnp.float32
BF16 = jnp.bfloat16

LANES = 128
D_MODEL = 1024
STREAM_CHUNK = 64
N_FOX_HEADS = 8
FOX_HEAD_DIM = 64
FOX_WIDTH = N_FOX_HEADS * FOX_HEAD_DIM
HEADS_PER_STEP = LANES // FOX_HEAD_DIM
GMLP_WIDTH = D_MODEL - FOX_WIDTH
GMLP_GROUPS = 4
GMLP_GROUP_DIM = GMLP_WIDTH // GMLP_GROUPS
GMLP_CHUNK = 128
N_MEM = 256
N_MEM_HEADS = 4
MEM_HEAD_DIM = D_MODEL // N_MEM_HEADS
EPS = 1e-5
LOG2E = 1.4426950408889634
DECAY_TERMS = 3
DECAY_ROWS = 16
assert HEADS_PER_STEP * DECAY_TERMS <= DECAY_ROWS
N_TILE_STATS = 4
PRUNE_LOG2_GAP = 160.0
VMEM_LIMIT = 56 * 1024 * 1024

PROMPT_ROW_TILE = 512
FOX_BLOCK = 512
KV_UNROLL = 4
TAIL_ROW_TILE = 512
TAIL_PART_MIN_ROWS = 128
FFN_ROW_TILE = 512
FFN_COL_CHUNK = 512
SAMPLE_KEY_CHUNK = 512


def _dot(a, b):
    return jnp.dot(a, b, preferred_element_type=F32)


def _dot_nt(a, b):
    return lax.dot_general(a, b, (((1,), (1,)), ((), ())), preferred_element_type=F32)


def _layer_norm(x, g, b):
    mu = jnp.mean(x, axis=-1, keepdims=True)
    xc = x - mu
    var = jnp.mean(xc * xc, axis=-1, keepdims=True)
    return xc * lax.rsqrt(var + EPS) * g + b


def _rms_norm(x, g):
    return x * lax.rsqrt(jnp.mean(x * x, axis=-1, keepdims=True) + EPS) * g


def _lane_cumsum(x, pos, length):
    shift = 1
    while shift < length:
        x = x + jnp.where(pos >= shift, pltpu.roll(x, shift, axis=1), 0.0)
        shift *= 2
    return x


def _div_pow2(x, n):
    assert n & (n - 1) == 0
    return x >> (n.bit_length() - 1)


def _params(*semantics):
    return pltpu.CompilerParams(dimension_semantics=semantics, vmem_limit_bytes=VMEM_LIMIT)


def _in_proj_kernel(*refs, tm, chunk, seg, prompt):
    (x_ref, wq_ref, wk_ref, wv_ref, wf_ref, bf_ref, wg_ref, lng_ref, lnb_ref,
     ws_ref, bs_ref, gout_ref) = refs[:12]
    if prompt:
        k_ref, v_ref, lf_ref, hg_ref, qt_ref, kx_ref, vxt_ref, st_ref, carry_ref = refs[12:]
    else:
        k_ref, v_ref, lf_ref, hg_ref, q_ref, ct_ref, gv_ref = refs[12:]

    if prompt:
        @pl.when(pl.program_id(1) == 0)
        def _():
            carry_ref[...] = jnp.zeros_like(carry_ref)

    xb = x_ref[0].astype(BF16)
    zf = _dot(xb, wf_ref[...]) + bf_ref[...]
    zg = _dot(xb, wg_ref[...])
    zq = _dot(xb, wq_ref[...])
    if prompt:
        qtb = (zq * (FOX_HEAD_DIM ** -0.5 * LOG2E)).T.astype(BF16)
        qt_ref[0] = qtb
    else:
        q_ref[0] = (zq * (FOX_HEAD_DIM ** -0.5)).astype(BF16)
    zk = _dot(xb, wk_ref[...])
    zv = _dot(xb, wv_ref[...])
    lf = jax.nn.log_sigmoid(zf)
    lft = lf.T[:N_FOX_HEADS, :]
    if prompt:
        zvt = zv.T
        zkt = zk.T
        k_ref[0] = zkt
        v_ref[0] = zvt
        lf_ref[0] = lft
    else:
        k_ref[0] = zk
        v_ref[0] = zv
        lf_ref[0] = lf[:, :N_FOX_HEADS]
    lane = lax.broadcasted_iota(jnp.int32, (N_FOX_HEADS, tm), 1)
    if prompt:
        c = _lane_cumsum(lft, lane, tm) + carry_ref[...]
        carry_ref[...] = jnp.broadcast_to(c[:, tm - 1:tm], (N_FOX_HEADS, tm))

        decay = c * (-LOG2E)

        def max_head_norms(xt):
            sq = xt * xt
            hid = lax.broadcasted_iota(jnp.int32, (N_FOX_HEADS, LANES), 0)
            out = jnp.zeros((N_FOX_HEADS, LANES), F32)
            for hd in range(N_FOX_HEADS):
                n2 = jnp.sum(sq[hd * FOX_HEAD_DIM:(hd + 1) * FOX_HEAD_DIM, :], axis=0, keepdims=True)
                out = jnp.where(hid == hd, jnp.max(jnp.sqrt(n2), axis=1, keepdims=True), out)
            return out

        st_ref[0, 0, 0 * N_FOX_HEADS:1 * N_FOX_HEADS, :] = max_head_norms(qtb.astype(F32))
        st_ref[0, 0, 1 * N_FOX_HEADS:2 * N_FOX_HEADS, :] = max_head_norms(zkt.astype(BF16).astype(F32))
        st_ref[0, 0, 2 * N_FOX_HEADS:3 * N_FOX_HEADS, :] = jnp.broadcast_to(decay[:, 0:1], (N_FOX_HEADS, LANES))
        st_ref[0, 0, 3 * N_FOX_HEADS:4 * N_FOX_HEADS, :] = jnp.broadcast_to(decay[:, tm - 1:tm], (N_FOX_HEADS, LANES))

        rest = decay
        terms = []
        for _ in range(DECAY_TERMS):
            t = rest.astype(BF16).astype(F32)
            terms.append(t)
            rest = rest - t
        rid = lax.broadcasted_iota(jnp.int32, (DECAY_ROWS, tm), 0)
        for g in range(FOX_WIDTH // LANES):
            dec = jnp.zeros((DECAY_ROWS, tm), F32)
            for hh in range(HEADS_PER_STEP):
                hd = g * HEADS_PER_STEP + hh
                for ti, t in enumerate(terms):
                    dec = jnp.where(rid == hh * DECAY_TERMS + ti, t[hd:hd + 1, :], dec)
            dec = jnp.concatenate([dec, jnp.zeros((LANES - DECAY_ROWS, tm), F32)], axis=0)
            kx_ref[0, :, 2 * g * LANES:(2 * g + 1) * LANES] = zk[:, g * LANES:(g + 1) * LANES].astype(BF16)
            kx_ref[0, :, (2 * g + 1) * LANES:(2 * g + 2) * LANES] = dec.T.astype(BF16)
        for hd in range(N_FOX_HEADS):
            vxt_ref[0, hd * LANES:hd * LANES + FOX_HEAD_DIM, :] = (
                zvt[hd * FOX_HEAD_DIM:(hd + 1) * FOX_HEAD_DIM, :].astype(BF16))
            vxt_ref[0, hd * LANES + FOX_HEAD_DIM:(hd + 1) * LANES, :] = jnp.ones((LANES - FOX_HEAD_DIM, tm), BF16)
    else:
        ct_ref[0] = _lane_cumsum(lft, lane & (seg - 1), seg)

    z = jax.nn.gelu(zg)
    u = z[:, :GMLP_WIDTH]
    vn = _layer_norm(z[:, GMLP_WIDTH:], lng_ref[...], lnb_ref[...])
    if not prompt:
        gv_ref[0] = vn
    vnb = vn.astype(BF16)
    ri = lax.broadcasted_iota(jnp.int32, (chunk, chunk), 0)
    ci = lax.broadcasted_iota(jnp.int32, (chunk, chunk), 1)
    causal = _div_pow2(ci, STREAM_CHUNK) <= _div_pow2(ri, STREAM_CHUNK)
    wm = [jnp.where(causal, ws_ref[g], 0.0).astype(BF16) for g in range(GMLP_GROUPS)]
    for r0 in range(0, tm, chunk):
        mixed = jnp.concatenate(
            [_dot(wm[g], vnb[r0:r0 + chunk, g * GMLP_GROUP_DIM:(g + 1) * GMLP_GROUP_DIM])
             for g in range(GMLP_GROUPS)], axis=1) + bs_ref[...]
        gm = u[r0:r0 + chunk, :] * mixed
        hg_ref[0, r0:r0 + chunk, :] = _rms_norm(gm, gout_ref[...]).astype(BF16)


def _in_proj_mix(x, wq, wk, wv, wf, bfp, wg, lng, lnb, ws, bs_full, gout, *, tm, chunk, seg, prompt):
    G, R, D = x.shape
    grid = (G, R // tm)
    row = lambda w: pl.BlockSpec((1, tm, w), lambda g, j: (g, j, 0))
    full = lambda a: pl.BlockSpec(a.shape, lambda g, j: (0,) * a.ndim)
    in_specs = [row(D)] + [full(a) for a in (wq, wk, wv, wf, bfp, wg, lng, lnb, ws, bs_full, gout)]
    col = lambda h: pl.BlockSpec((1, h, tm), lambda g, j: (g, 0, j))
    wide = (lambda w: (G, w, R)) if prompt else (lambda w: (G, R, w))
    spec = col if prompt else row
    out_shape = [
        jax.ShapeDtypeStruct(wide(FOX_WIDTH), F32),
        jax.ShapeDtypeStruct(wide(FOX_WIDTH), F32),
        jax.ShapeDtypeStruct(wide(N_FOX_HEADS), F32),
        jax.ShapeDtypeStruct((G, R, GMLP_WIDTH), BF16),
    ]
    out_specs = [spec(FOX_WIDTH), spec(FOX_WIDTH), spec(N_FOX_HEADS), row(GMLP_WIDTH)]
    scratch = []
    if prompt:
        out_shape += [
            jax.ShapeDtypeStruct((G, FOX_WIDTH, R), BF16),
            jax.ShapeDtypeStruct((G, R, 2 * FOX_WIDTH), BF16),
            jax.ShapeDtypeStruct((G, N_FOX_HEADS * LANES, R), BF16),
            jax.ShapeDtypeStruct((G, R // tm, N_TILE_STATS * N_FOX_HEADS, LANES), F32)]
        out_specs += [col(FOX_WIDTH), row(2 * FOX_WIDTH), col(N_FOX_HEADS * LANES),
                      pl.BlockSpec((1, 1, N_TILE_STATS * N_FOX_HEADS, LANES), lambda g, j: (g, j, 0, 0))]
        scratch = [pltpu.VMEM((N_FOX_HEADS, tm), F32)]
    else:
        out_shape += [jax.ShapeDtypeStruct((G, R, FOX_WIDTH), BF16),
                      jax.ShapeDtypeStruct((G, N_FOX_HEADS, R), F32),
                      jax.ShapeDtypeStruct((G, R, GMLP_WIDTH), F32)]
        out_specs += [row(FOX_WIDTH), col(N_FOX_HEADS), row(GMLP_WIDTH)]
    return pl.pallas_call(
        functools.partial(_in_proj_kernel, tm=tm, chunk=chunk, seg=seg, prompt=prompt),
        grid=grid, in_specs=in_specs, out_specs=out_specs, out_shape=out_shape,
        scratch_shapes=scratch, compiler_params=_params("arbitrary", "arbitrary"),
        name="in_proj_mix_prompt" if prompt else "in_proj_mix_sample",
    )(x, wq, wk, wv, wf, bfp, wg, lng, lnb, ws, bs_full, gout)


def _fox_prompt_kernel(first_ref, qt_ref, kx_ref, vxt_ref, o_ref, qx_ref, m_ref, acc_ref, s_ref, bm_ref, *, tb):
    qi = pl.program_id(2)
    first = first_ref[(pl.program_id(0) * pl.num_programs(1) + pl.program_id(1)) * pl.num_programs(2) + qi]
    dim = lax.broadcasted_iota(jnp.int32, (LANES, tb), 0)
    key = lax.broadcasted_iota(jnp.int32, (tb, tb), 0)
    qry = lax.broadcasted_iota(jnp.int32, (tb, tb), 1)
    qt = qt_ref[0]
    for h in range(HEADS_PER_STEP):
        own = _div_pow2(dim, FOX_HEAD_DIM) == h
        decay = (dim >= h * DECAY_TERMS) & (dim < (h + 1) * DECAY_TERMS)
        qx_ref[h, :LANES, :] = jnp.where(own, qt, jnp.zeros_like(qt))
        qx_ref[h, LANES:, :] = jnp.where(decay, 1.0, 0.0).astype(BF16)
    m_ref[...] = jnp.full_like(m_ref, -jnp.inf)
    acc_ref[...] = jnp.zeros_like(acc_ref)

    def scores(j, keep):
        kx = kx_ref[0, pl.ds(pl.multiple_of(j * tb, tb), tb), :]
        out = []
        for h in range(HEADS_PER_STEP):
            s = _dot(kx, qx_ref[h])
            out.append(s if keep is None else jnp.where(keep, s, -jnp.inf))
        return out

    def stash(s_new):
        for h in range(HEADS_PER_STEP):
            s_ref[h] = s_new[h]
            bm_ref[h] = jnp.max(s_new[h], axis=0, keepdims=True)

    def consume(j):
        start = pl.multiple_of(j * tb, tb)
        for h in range(HEADS_PER_STEP):
            m_prev = m_ref[h]
            m_new = jnp.maximum(m_prev, bm_ref[h])
            p = jnp.exp2(s_ref[h] - m_new)
            pv = _dot(vxt_ref[0, h * LANES:(h + 1) * LANES, pl.ds(start, tb)], p.astype(BF16))
            acc_ref[h] = jnp.exp2(m_prev - m_new) * acc_ref[h] + pv
            m_ref[h] = m_new

    def pipelined(t):
        s_next = scores(t, None)
        consume(jnp.where(t == first, qi, t - 1))
        stash(s_next)

    stash(scores(qi, key <= qry))

    def unrolled(t, carry):
        for u in range(KV_UNROLL):
            pipelined(first + KV_UNROLL * t + u)
        return carry

    def single(t, carry):
        pipelined(t)
        return carry

    n_unrolled = _div_pow2(qi - first, KV_UNROLL)
    lax.fori_loop(0, n_unrolled, unrolled, 0)
    lax.fori_loop(first + n_unrolled * KV_UNROLL, qi, single, 0)
    consume(jnp.where(qi > first, qi - 1, qi))
    outs = []
    for h in range(HEADS_PER_STEP):
        acc = acc_ref[h]
        outs.append(acc[:FOX_HEAD_DIM, :] / acc[FOX_HEAD_DIM:FOX_HEAD_DIM + 1, :])
    o_ref[0] = jnp.concatenate(outs, axis=0).T


def _first_live_block(stats):
    st = stats[..., 0]
    B, nblk, _ = st.shape
    qn, kn, d_first, d_last = (st[:, :, i * N_FOX_HEADS:(i + 1) * N_FOX_HEADS] for i in range(N_TILE_STATS))
    bound = (qn[:, :, None] * (kn[:, None, :] + kn[:, :, None])
             + d_last[:, None, :] - d_first[:, :, None])
    dead = bound < -PRUNE_LOG2_GAP
    dead = dead.reshape(B, nblk, nblk, N_FOX_HEADS // HEADS_PER_STEP, HEADS_PER_STEP).all(axis=-1)
    below = jnp.arange(nblk)[None, :] < jnp.arange(nblk)[:, None]
    dead = jnp.logical_and(dead, below[None, :, :, None])
    first = jnp.sum(jnp.cumprod(dead.astype(jnp.int32), axis=2), axis=2)
    return first.transpose(0, 2, 1).reshape(-1)


def _fox_prompt(qt, kx, vxt, stats, *, tb):
    B, _, S = qt.shape
    assert stats.shape[1] == S // tb
    return pl.pallas_call(
        functools.partial(_fox_prompt_kernel, tb=tb),
        grid_spec=pltpu.PrefetchScalarGridSpec(
            num_scalar_prefetch=1,
            grid=(B, FOX_WIDTH // LANES, S // tb),
            in_specs=[
                pl.BlockSpec((1, LANES, tb), lambda b, g, i, first: (b, g, i)),
                pl.BlockSpec((1, S, 2 * LANES), lambda b, g, i, first: (b, 0, g)),
                pl.BlockSpec((1, 2 * LANES, S), lambda b, g, i, first: (b, g, 0)),
            ],
            out_specs=pl.BlockSpec((1, tb, LANES), lambda b, g, i, first: (b, i, g)),
            scratch_shapes=[pltpu.VMEM((HEADS_PER_STEP, 2 * LANES, tb), BF16),
                            pltpu.VMEM((HEADS_PER_STEP, 1, tb), F32),
                            pltpu.VMEM((HEADS_PER_STEP, LANES, tb), F32),
                            pltpu.VMEM((HEADS_PER_STEP, tb, tb), F32),
                            pltpu.VMEM((HEADS_PER_STEP, 1, tb), F32)]),
        out_shape=jax.ShapeDtypeStruct((B, S, FOX_WIDTH), F32),
        compiler_params=_params("arbitrary", "arbitrary", "arbitrary"),
        name="fox_prompt",
    )(_first_live_block(stats), qt, kx, vxt)


def _fox_sample_kernel(q_ref, kn_ref, vn_ref, cn_ref, ck_ref, cv_ref, lft_ref, o_ref, s_ref, *, P, T):
    rows = N_FOX_HEADS * T
    q = q_ref[0]
    qt = jnp.concatenate([q] * N_FOX_HEADS, axis=0)
    r = lax.broadcasted_iota(jnp.int32, (rows, FOX_WIDTH), 0)
    c = lax.broadcasted_iota(jnp.int32, (rows, FOX_WIDTH), 1)
    wt = jnp.where(_div_pow2(r, T) == _div_pow2(c, FOX_HEAD_DIM), qt, jnp.zeros_like(qt))

    lane = lax.broadcasted_iota(jnp.int32, (N_FOX_HEADS, P), 1)
    cs = _lane_cumsum(lft_ref[0], lane, P)
    suffix = cs[:, P - 1:P] - cs

    def head_rows(x):
        return jnp.concatenate(
            [jnp.broadcast_to(x[h:h + 1, :], (T, x.shape[1])) for h in range(N_FOX_HEADS)], axis=0)

    kc_w = SAMPLE_KEY_CHUNK
    for k0 in range(0, P, kc_w):
        kct = ck_ref[0, :, k0:k0 + kc_w].astype(BF16)
        s_ref[:, k0:k0 + kc_w] = _dot(wt, kct) + head_rows(suffix[:, k0:k0 + kc_w])
    pad = jnp.zeros((LANES - T, FOX_WIDTH), BF16)
    sn = _dot_nt(wt, jnp.concatenate([kn_ref[0].astype(BF16), pad], axis=0))
    rr = lax.broadcasted_iota(jnp.int32, (rows, LANES), 0)
    cc = lax.broadcasted_iota(jnp.int32, (rows, LANES), 1)
    s_ref[:, P:P + LANES] = jnp.where(cc <= (rr & (T - 1)), sn - head_rows(cn_ref[0]), -jnp.inf)

    n_cols = P + LANES
    m = jnp.full((rows, 1), -jnp.inf, F32)
    for k0 in range(0, n_cols, kc_w):
        w = min(kc_w, n_cols - k0)
        m = jnp.maximum(m, jnp.max(s_ref[:, k0:k0 + w], axis=1, keepdims=True))
    l = jnp.zeros((rows, 1), F32)
    acc = jnp.zeros((rows, FOX_WIDTH), F32)
    for k0 in range(0, P, kc_w):
        p = jnp.exp(s_ref[:, k0:k0 + kc_w] - m)
        l = l + jnp.sum(p, axis=1, keepdims=True)
        acc = acc + _dot_nt(p.astype(BF16), cv_ref[0, :, k0:k0 + kc_w].astype(BF16))
    p = jnp.exp(s_ref[:, P:P + LANES] - m)
    l = l + jnp.sum(p, axis=1, keepdims=True)
    acc = acc + _dot(p.astype(BF16), jnp.concatenate([vn_ref[0].astype(BF16), pad], axis=0))
    on = acc / l
    c16 = lax.broadcasted_iota(jnp.int32, (T, FOX_WIDTH), 1)
    out = jnp.zeros((T, FOX_WIDTH), F32)
    for h in range(N_FOX_HEADS):
        out = out + jnp.where(_div_pow2(c16, FOX_HEAD_DIM) == h, on[h * T:(h + 1) * T, :], 0.0)
    o_ref[0] = out


def _fox_sample(q, kn, vn, cn_pad, cache_k, cache_v, cache_lft):
    Bd, T, _ = q.shape
    P = cache_k.shape[2]
    per_b = lambda shape: pl.BlockSpec((1,) + shape, lambda b: (b, 0, 0))
    return pl.pallas_call(
        functools.partial(_fox_sample_kernel, P=P, T=T),
        grid=(Bd,),
        in_specs=[per_b((T, FOX_WIDTH)), per_b((T, FOX_WIDTH)), per_b((T, FOX_WIDTH)),
                  per_b((N_FOX_HEADS, LANES)), per_b((FOX_WIDTH, P)), per_b((FOX_WIDTH, P)),
                  per_b((N_FOX_HEADS, P))],
        out_specs=per_b((T, FOX_WIDTH)),
        out_shape=jax.ShapeDtypeStruct((Bd, T, FOX_WIDTH), F32),
        scratch_shapes=[pltpu.VMEM((N_FOX_HEADS * T, P + LANES), F32)],
        compiler_params=_params("arbitrary"),
        name="fox_sample",
    )(q, kn, vn, cn_pad, cache_k, cache_v, cache_lft)


def _memory_kv_kernel(mem_ref, wk_ref, wv_ref, k_ref, v_ref):
    mb = mem_ref[...].astype(BF16)
    k_ref[...] = _dot(mb, wk_ref[...])
    v_ref[...] = _dot(mb, wv_ref[...])


def _memory_kv(mem2d, wk, wv):
    n = mem2d.shape[0]
    full = lambda a: pl.BlockSpec(a.shape, lambda i: (0, 0))
    out = jax.ShapeDtypeStruct((n, D_MODEL), F32)
    return pl.pallas_call(
        _memory_kv_kernel, grid=(1,),
        in_specs=[full(mem2d), full(wk), full(wv)],
        out_specs=[pl.BlockSpec((n, D_MODEL), lambda i: (0, 0))] * 2,
        out_shape=[out, out], compiler_params=_params("arbitrary"), name="memory_kv",
    )(mem2d, wk, wv)


def _merge_xattn_kernel(x_ref, att_ref, hg_ref, mk_ref, mv_ref, gfox_ref, wo_ref, ln1g_ref, ln1b_ref,
                        wmq_ref, wmo_ref, ln2g_ref, ln2b_ref, o_ref, *, alpha):
    tm = x_ref.shape[1]
    n_parts = 2 if tm % (2 * TAIL_PART_MIN_ROWS) == 0 else 1
    parts = [slice(i * tm // n_parts, (i + 1) * tm // n_parts) for i in range(n_parts)]
    head = lambda hh: slice(hh * MEM_HEAD_DIM, (hh + 1) * MEM_HEAD_DIM)
    mk = mk_ref[0].astype(BF16)
    mv = mv_ref[0].astype(BF16)
    h = [jnp.concatenate([_rms_norm(att_ref[0, r, :], gfox_ref[...]).astype(BF16), hg_ref[0, r, :]], axis=1)
         for r in parts]
    mix = [_dot(hi, wo_ref[...]) for hi in h]
    x1 = [_layer_norm(alpha * x_ref[0, r, :] + mi, ln1g_ref[...], ln1b_ref[...]) for r, mi in zip(parts, mix)]
    qm = [_dot(xi.astype(BF16), wmq_ref[...]) for xi in x1]
    qm = [(qi * (MEM_HEAD_DIM ** -0.5)).astype(BF16) for qi in qm]
    s = [[_dot_nt(qi[:, head(hh)], mk[:, head(hh)]) for hh in range(N_MEM_HEADS)] for qi in qm]
    p = [[jnp.exp(sh - jnp.max(sh, axis=1, keepdims=True)) for sh in si] for si in s]
    o = [[_dot(ph.astype(BF16), mv[:, head(hh)]) / jnp.sum(ph, axis=1, keepdims=True)
          for hh, ph in enumerate(pi)] for pi in p]
    y = [_dot(jnp.concatenate(oi, axis=1).astype(BF16), wmo_ref[...]) for oi in o]
    for r, xi, yi in zip(parts, x1, y):
        o_ref[0, r, :] = _layer_norm(alpha * xi + yi, ln2g_ref[...], ln2b_ref[...])


def _merge_xattn(x, att, hg, mk, mv, gfox, wo, ln1g, ln1b, wmq, wmo, ln2g, ln2b, *, tm, alpha):
    G, R, D = x.shape
    row = lambda w: pl.BlockSpec((1, tm, w), lambda g, j: (g, j, 0))
    mem = pl.BlockSpec((1, N_MEM, D), lambda g, j: (g, 0, 0))
    full = lambda a: pl.BlockSpec(a.shape, lambda g, j: (0,) * a.ndim)
    return pl.pallas_call(
        functools.partial(_merge_xattn_kernel, alpha=alpha),
        grid=(G, R // tm),
        in_specs=[row(D), row(FOX_WIDTH), row(GMLP_WIDTH), mem, mem] +
                 [full(a) for a in (gfox, wo, ln1g, ln1b, wmq, wmo, ln2g, ln2b)],
        out_specs=row(D), out_shape=jax.ShapeDtypeStruct((G, R, D), F32),
        compiler_params=_params("arbitrary", "arbitrary"), name=f"merge_xattn_{tm}",
    )(x, att, hg, mk, mv, gfox, wo, ln1g, ln1b, wmq, wmo, ln2g, ln2b)


def _ffn_kernel(x_ref, wg_ref, wu_ref, wd_ref, g_ref, b_ref, o_ref, h_ref, *, alpha, d_ff):
    tm = x_ref.shape[1]
    n_parts = 2 if tm % (2 * TAIL_PART_MIN_ROWS) == 0 else 1
    parts = [slice(i * tm // n_parts, (i + 1) * tm // n_parts) for i in range(n_parts)]
    xb = [x_ref[0, r, :].astype(BF16) for r in parts]
    for c0 in range(0, d_ff, FFN_COL_CHUNK):
        cols = slice(c0, min(c0 + FFN_COL_CHUNK, d_ff))
        for r, xi in zip(parts, xb):
            gate = _dot(xi, wg_ref[:, cols])
            up = _dot(xi, wu_ref[:, cols])
            h_ref[r, cols] = (jax.nn.silu(gate) * up).astype(BF16)
    y = [_dot(h_ref[r, :], wd_ref[...]) for r in parts]
    for r, yi in zip(parts, y):
        o_ref[0, r, :] = _layer_norm(alpha * x_ref[0, r, :] + yi, g_ref[...], b_ref[...])


def _ffn(x, wg, wu, wd, g, b, *, tm, alpha):
    G, R, D = x.shape
    d_ff = wg.shape[1]
    assert d_ff % LANES == 0
    row = pl.BlockSpec((1, tm, D), lambda gi, j: (gi, j, 0))
    full = lambda a: pl.BlockSpec(a.shape, lambda gi, j: (0,) * a.ndim)
    return pl.pallas_call(
        functools.partial(_ffn_kernel, alpha=alpha, d_ff=d_ff),
        grid=(G, R // tm),
        in_specs=[row] + [full(a) for a in (wg, wu, wd, g, b)],
        out_specs=row, out_shape=jax.ShapeDtypeStruct((G, R, D), F32),
        scratch_shapes=[pltpu.VMEM((tm, d_ff), BF16)],
        compiler_params=_params("arbitrary", "arbitrary"), name=f"ffn_{tm}",
    )(x, wg, wu, wd, g, b)


def _row(a):
    return a.reshape(1, -1)


def kernel(x_prompt, x_sample, cache_fox_k, cache_fox_v, cache_fox_logf, cache_mem_k, cache_mem_v, mem_prompt, w_in, b_f, g_fox_out, g_gmlp_out, sgu_ln_g, sgu_ln_b, w_s, b_s, w_o, ln1_g, ln1_b, w_mq, w_mk, w_mv, w_mo, ln2_g, ln2_b, w_gate, w_up, w_down, ln3_g, ln3_b):
    depth = w_in.shape[0]
    B, S, D = x_prompt.shape
    Bd, T, _ = x_sample.shape
    P = cache_fox_k.shape[2]
    alpha = (2.0 * depth) ** 0.25
    assert D == D_MODEL and S % FOX_BLOCK == 0 and S % PROMPT_ROW_TILE == 0
    assert PROMPT_ROW_TILE % GMLP_CHUNK == 0 and (T & (T - 1)) == 0 and N_FOX_HEADS * T == LANES
    assert P % SAMPLE_KEY_CHUNK == 0 and S % TAIL_ROW_TILE == 0 and S % FFN_ROW_TILE == 0

    yp, ys = x_prompt, x_sample
    outs = [[] for _ in range(9)]
    for l in range(depth):
        w = w_in[l].astype(BF16)
        wq, wk, wv = (w[:, i * FOX_WIDTH:(i + 1) * FOX_WIDTH] for i in range(3))
        f0 = 3 * FOX_WIDTH
        wf = jnp.pad(w[:, f0:f0 + N_FOX_HEADS], ((0, 0), (0, LANES - N_FOX_HEADS)))
        bfp = jnp.pad(_row(b_f[l]), ((0, 0), (0, LANES - N_FOX_HEADS)))
        wg = w[:, f0 + N_FOX_HEADS:]
        lng, lnb, gout = _row(sgu_ln_g[l]), _row(sgu_ln_b[l]), _row(g_gmlp_out[l])
        tail_w = (_row(g_fox_out[l]), w_o[l].astype(BF16), _row(ln1_g[l]), _row(ln1_b[l]),
                  w_mq[l].astype(BF16), w_mo[l].astype(BF16), _row(ln2_g[l]), _row(ln2_b[l]))
        ffn_w = (w_gate[l].astype(BF16), w_up[l].astype(BF16), w_down[l].astype(BF16),
                 _row(ln3_g[l]), _row(ln3_b[l]))

        def mix_params(n):
            bias = jnp.repeat(b_s[l][:, :n].T, GMLP_GROUP_DIM, axis=1)
            return w_s[l][:, :n, :n], bias

        ws_p, bs_p = mix_params(GMLP_CHUNK)
        k, v, lf, hg, qt, kx, vxt, stats = _in_proj_mix(
            yp, wq, wk, wv, wf, bfp, wg, lng, lnb, ws_p, bs_p, gout,
            tm=PROMPT_ROW_TILE, chunk=GMLP_CHUNK, seg=None, prompt=True)
        att = _fox_prompt(qt, kx, vxt, stats, tb=FOX_BLOCK)
        mk, mv = _memory_kv(mem_prompt.reshape(B * N_MEM, D), w_mk[l].astype(BF16), w_mv[l].astype(BF16))
        mk, mv = mk.reshape(B, N_MEM, D), mv.reshape(B, N_MEM, D)
        x2 = _merge_xattn(yp, att, hg, mk, mv, *tail_w, tm=TAIL_ROW_TILE, alpha=alpha)
        yp = _ffn(x2, *ffn_w, tm=FFN_ROW_TILE, alpha=alpha)
        heads_last = lambda a: a.reshape(B, N_FOX_HEADS, FOX_HEAD_DIM, S).transpose(0, 3, 1, 2)
        outs[0].append(heads_last(k))
        outs[1].append(heads_last(v))
        outs[2].append(lf.transpose(0, 2, 1))
        outs[3].append(mk.reshape(B, N_MEM, N_MEM_HEADS, MEM_HEAD_DIM))
        outs[4].append(mv.reshape(B, N_MEM, N_MEM_HEADS, MEM_HEAD_DIM))

        ws_s, bs_s = mix_params(T)
        k, v, lf, hg, q, ct, gv = _in_proj_mix(
            ys.reshape(1, Bd * T, D), wq, wk, wv, wf, bfp, wg, lng, lnb, ws_s, bs_s, gout,
            tm=Bd * T, chunk=T, seg=T, prompt=False)
        per_b = lambda a: a.reshape(Bd, T, a.shape[-1])
        cn = ct.reshape(N_FOX_HEADS, Bd, T).transpose(1, 0, 2)
        cn = jnp.pad(cn, ((0, 0), (0, 0), (0, LANES - T)))
        dims_first = lambda a: a.transpose(0, 2, 3, 1).reshape(Bd, FOX_WIDTH, P)
        att = _fox_sample(per_b(q), per_b(k), per_b(v), cn,
                          dims_first(cache_fox_k[l]), dims_first(cache_fox_v[l]),
                          jnp.swapaxes(cache_fox_logf[l], 1, 2))
        x2 = _merge_xattn(ys, att, per_b(hg), cache_mem_k[l].reshape(Bd, N_MEM, D),
                          cache_mem_v[l].reshape(Bd, N_MEM, D), *tail_w, tm=T, alpha=alpha)
        ys = _ffn(x2.reshape(1, Bd * T, D), *ffn_w, tm=Bd * T, alpha=alpha).reshape(Bd, T, D)
        outs[5].append(k.reshape(Bd, T, N_FOX_HEADS, FOX_HEAD_DIM))
        outs[6].append(v.reshape(Bd, T, N_FOX_HEADS, FOX_HEAD_DIM))
        outs[7].append(per_b(lf))
        outs[8].append(per_b(gv))
    return (yp, ys) + tuple(jnp.stack(o) for o in outs)
```

```python
import functools

import jax
import jax.numpy as jnp
from jax import lax
from jax.experimental import pallas as pl
from jax.experimental.pallas import tpu as pltpu

F32 = jnp.float32
BF16 = jnp.bfloat16

LANES = 128
D_MODEL = 1024
STREAM_CHUNK = 64
N_FOX_HEADS = 8
FOX_HEAD_DIM = 64
FOX_WIDTH = N_FOX_HEADS * FOX_HEAD_DIM
HEADS_PER_STEP = LANES // FOX_HEAD_DIM
GMLP_WIDTH = D_MODEL - FOX_WIDTH
GMLP_GROUPS = 4
GMLP_GROUP_DIM = GMLP_WIDTH // GMLP_GROUPS
GMLP_CHUNK = 128
N_MEM = 256
N_MEM_HEADS = 4
MEM_HEAD_DIM = D_MODEL // N_MEM_HEADS
EPS = 1e-5
LOG2E = 1.4426950408889634
DECAY_TERMS = 3
DECAY_ROWS = 16
assert HEADS_PER_STEP * DECAY_TERMS <= DECAY_ROWS
N_TILE_STATS = 4
PRUNE_LOG2_GAP = 160.0
VMEM_LIMIT = 56 * 1024 * 1024

PROMPT_ROW_TILE = 512
FOX_BLOCK = 512
KV_UNROLL = 4
FOX_Q_BLOCKS = 4
TAIL_ROW_TILE = 1024
TAIL_PART_ROWS = 256
FFN_ROW_TILE = 512
FFN_COL_CHUNK = 512
SAMPLE_KEY_CHUNK = 512


def _dot(a, b):
    return jnp.dot(a, b, preferred_element_type=F32)


def _dot_nt(a, b):
    return lax.dot_general(a, b, (((1,), (1,)), ((), ())), preferred_element_type=F32)


def _layer_norm(x, g, b):
    mu = jnp.mean(x, axis=-1, keepdims=True)
    xc = x - mu
    var = jnp.mean(xc * xc, axis=-1, keepdims=True)
    return xc * lax.rsqrt(var + EPS) * g + b


def _rms_norm(x, g):
    return x * lax.rsqrt(jnp.mean(x * x, axis=-1, keepdims=True) + EPS) * g


def _lane_cumsum(x, pos, length):
    shift = 1
    while shift < length:
        x = x + jnp.where(pos >= shift, pltpu.roll(x, shift, axis=1), 0.0)
        shift *= 2
    return x


def _div_pow2(x, n):
    assert n & (n - 1) == 0
    return x >> (n.bit_length() - 1)


def _row_parts(tm):
    n = tm // TAIL_PART_ROWS if tm % TAIL_PART_ROWS == 0 else 1
    return [slice(i * tm // n, (i + 1) * tm // n) for i in range(n)]


def _params(*semantics):
    return pltpu.CompilerParams(dimension_semantics=semantics, vmem_limit_bytes=VMEM_LIMIT)


def _in_proj_kernel(*refs, tm, chunk, seg, prompt):
    (x_ref, wq_ref, wk_ref, wv_ref, wf_ref, bf_ref, wg_ref, lng_ref, lnb_ref,
     ws_ref, bs_ref, gout_ref) = refs[:12]
    if prompt:
        k_ref, v_ref, lf_ref, hg_ref, qt_ref, kx_ref, vxt_ref, st_ref, carry_ref = refs[12:]
    else:
        k_ref, v_ref, lf_ref, hg_ref, q_ref, ct_ref, gv_ref = refs[12:]

    if prompt:
        @pl.when(pl.program_id(1) == 0)
        def _():
            carry_ref[...] = jnp.zeros_like(carry_ref)

    xb = x_ref[0].astype(BF16)
    zf = _dot(xb, wf_ref[...]) + bf_ref[...]
    zg = _dot(xb, wg_ref[...])
    zq = _dot(xb, wq_ref[...])
    if prompt:
        qtb = (zq * (FOX_HEAD_DIM ** -0.5 * LOG2E)).T.astype(BF16)
        qt_ref[0] = qtb
    else:
        q_ref[0] = (zq * (FOX_HEAD_DIM ** -0.5)).astype(BF16)
    zk = _dot(xb, wk_ref[...])
    zv = _dot(xb, wv_ref[...])
    lf = jax.nn.log_sigmoid(zf)
    lft = lf.T[:N_FOX_HEADS, :]
    if prompt:
        zvt = zv.T
        zkt = zk.T
        k_ref[0] = zkt
        v_ref[0] = zvt
        lf_ref[0] = lft
    else:
        k_ref[0] = zk
        v_ref[0] = zv
        lf_ref[0] = lf[:, :N_FOX_HEADS]
    lane = lax.broadcasted_iota(jnp.int32, (N_FOX_HEADS, tm), 1)
    if prompt:
        c = _lane_cumsum(lft, lane, tm) + carry_ref[...]
        carry_ref[...] = jnp.broadcast_to(c[:, tm - 1:tm], (N_FOX_HEADS, tm))

        decay = c * (-LOG2E)

        def max_head_norms(xt):
            sq = xt * xt
            hid = lax.broadcasted_iota(jnp.int32, (N_FOX_HEADS, LANES), 0)
            out = jnp.zeros((N_FOX_HEADS, LANES), F32)
            for hd in range(N_FOX_HEADS):
                n2 = jnp.sum(sq[hd * FOX_HEAD_DIM:(hd + 1) * FOX_HEAD_DIM, :], axis=0, keepdims=True)
                out = jnp.where(hid == hd, jnp.max(jnp.sqrt(n2), axis=1, keepdims=True), out)
            return out

        st_ref[0, 0, 0 * N_FOX_HEADS:1 * N_FOX_HEADS, :] = max_head_norms(qtb.astype(F32))
        st_ref[0, 0, 1 * N_FOX_HEADS:2 * N_FOX_HEADS, :] = max_head_norms(zkt.astype(BF16).astype(F32))
        st_ref[0, 0, 2 * N_FOX_HEADS:3 * N_FOX_HEADS, :] = jnp.broadcast_to(decay[:, 0:1], (N_FOX_HEADS, LANES))
        st_ref[0, 0, 3 * N_FOX_HEADS:4 * N_FOX_HEADS, :] = jnp.broadcast_to(decay[:, tm - 1:tm], (N_FOX_HEADS, LANES))

        rest = decay
        terms = []
        for _ in range(DECAY_TERMS):
            t = rest.astype(BF16).astype(F32)
            terms.append(t)
            rest = rest - t
        rid = lax.broadcasted_iota(jnp.int32, (DECAY_ROWS, tm), 0)
        for g in range(FOX_WIDTH // LANES):
            dec = jnp.zeros((DECAY_ROWS, tm), F32)
            for hh in range(HEADS_PER_STEP):
                hd = g * HEADS_PER_STEP + hh
                for ti, t in enumerate(terms):
                    dec = jnp.where(rid == hh * DECAY_TERMS + ti, t[hd:hd + 1, :], dec)
            dec = jnp.concatenate([dec, jnp.zeros((LANES - DECAY_ROWS, tm), F32)], axis=0)
            kx_ref[0, :, 2 * g * LANES:(2 * g + 1) * LANES] = zk[:, g * LANES:(g + 1) * LANES].astype(BF16)
            kx_ref[0, :, (2 * g + 1) * LANES:(2 * g + 2) * LANES] = dec.T.astype(BF16)
        for hd in range(N_FOX_HEADS):
            vxt_ref[0, hd * LANES:hd * LANES + FOX_HEAD_DIM, :] = (
                zvt[hd * FOX_HEAD_DIM:(hd + 1) * FOX_HEAD_DIM, :].astype(BF16))
            vxt_ref[0, hd * LANES + FOX_HEAD_DIM:(hd + 1) * LANES, :] = jnp.ones((LANES - FOX_HEAD_DIM, tm), BF16)
    else:
        ct_ref[0] = _lane_cumsum(lft, lane & (seg - 1), seg)

    z = jax.nn.gelu(zg)
    u = z[:, :GMLP_WIDTH]
    vn = _layer_norm(z[:, GMLP_WIDTH:], lng_ref[...], lnb_ref[...])
    if not prompt:
        gv_ref[0] = vn
    vnb = vn.astype(BF16)
    ri = lax.broadcasted_iota(jnp.int32, (chunk, chunk), 0)
    ci = lax.broadcasted_iota(jnp.int32, (chunk, chunk), 1)
    causal = _div_pow2(ci, STREAM_CHUNK) <= _div_pow2(ri, STREAM_CHUNK)
    wm = [jnp.where(causal, ws_ref[g], 0.0).astype(BF16) for g in range(GMLP_GROUPS)]
    for r0 in range(0, tm, chunk):
        mixed = jnp.concatenate(
            [_dot(wm[g], vnb[r0:r0 + chunk, g * GMLP_GROUP_DIM:(g + 1) * GMLP_GROUP_DIM])
             for g in range(GMLP_GROUPS)], axis=1) + bs_ref[...]
        gm = u[r0:r0 + chunk, :] * mixed
        hg_ref[0, r0:r0 + chunk, :] = _rms_norm(gm, gout_ref[...]).astype(BF16)


def _in_proj_mix(x, wq, wk, wv, wf, bfp, wg, lng, lnb, ws, bs_full, gout, *, tm, chunk, seg, prompt):
    G, R, D = x.shape
    grid = (G, R // tm)
    row = lambda w: pl.BlockSpec((1, tm, w), lambda g, j: (g, j, 0))
    full = lambda a: pl.BlockSpec(a.shape, lambda g, j: (0,) * a.ndim)
    in_specs = [row(D)] + [full(a) for a in (wq, wk, wv, wf, bfp, wg, lng, lnb, ws, bs_full, gout)]
    col = lambda h: pl.BlockSpec((1, h, tm), lambda g, j: (g, 0, j))
    wide = (lambda w: (G, w, R)) if prompt else (lambda w: (G, R, w))
    spec = col if prompt else row
    out_shape = [
        jax.ShapeDtypeStruct(wide(FOX_WIDTH), F32),
        jax.ShapeDtypeStruct(wide(FOX_WIDTH), F32),
        jax.ShapeDtypeStruct(wide(N_FOX_HEADS), F32),
        jax.ShapeDtypeStruct((G, R, GMLP_WIDTH), BF16),
    ]
    out_specs = [spec(FOX_WIDTH), spec(FOX_WIDTH), spec(N_FOX_HEADS), row(GMLP_WIDTH)]
    scratch = []
    if prompt:
        out_shape += [
            jax.ShapeDtypeStruct((G, FOX_WIDTH, R), BF16),
            jax.ShapeDtypeStruct((G, R, 2 * FOX_WIDTH), BF16),
            jax.ShapeDtypeStruct((G, N_FOX_HEADS * LANES, R), BF16),
            jax.ShapeDtypeStruct((G, R // tm, N_TILE_STATS * N_FOX_HEADS, LANES), F32)]
        out_specs += [col(FOX_WIDTH), row(2 * FOX_WIDTH), col(N_FOX_HEADS * LANES),
                      pl.BlockSpec((1, 1, N_TILE_STATS * N_FOX_HEADS, LANES), lambda g, j: (g, j, 0, 0))]
        scratch = [pltpu.VMEM((N_FOX_HEADS, tm), F32)]
    else:
        out_shape += [jax.ShapeDtypeStruct((G, R, FOX_WIDTH), BF16),
                      jax.ShapeDtypeStruct((G, N_FOX_HEADS, R), F32),
                      jax.ShapeDtypeStruct((G, R, GMLP_WIDTH), F32)]
        out_specs += [row(FOX_WIDTH), col(N_FOX_HEADS), row(GMLP_WIDTH)]
    return pl.pallas_call(
        functools.partial(_in_proj_kernel, tm=tm, chunk=chunk, seg=seg, prompt=prompt),
        grid=grid, in_specs=in_specs, out_specs=out_specs, out_shape=out_shape,
        scratch_shapes=scratch, compiler_params=_params("arbitrary", "arbitrary"),
        name="in_proj_mix_prompt" if prompt else "in_proj_mix_sample",
    )(x, wq, wk, wv, wf, bfp, wg, lng, lnb, ws, bs_full, gout)


def _fox_prompt_kernel(first_ref, qt_ref, kx_ref, vxt_ref, o_ref, qx_ref, m_ref, acc_ref, s_ref, bm_ref, *, tb):
    step = pl.program_id(2)
    flat = (pl.program_id(0) * pl.num_programs(1) + pl.program_id(1)) * pl.num_programs(2) + step
    dim = lax.broadcasted_iota(jnp.int32, (LANES, tb), 0)
    key = lax.broadcasted_iota(jnp.int32, (tb, tb), 0)
    qry = lax.broadcasted_iota(jnp.int32, (tb, tb), 1)
    causal = key <= qry
    for a in range(FOX_Q_BLOCKS):
        qt = qt_ref[0, :, a * tb:(a + 1) * tb]
        for h in range(HEADS_PER_STEP):
            own = _div_pow2(dim, FOX_HEAD_DIM) == h
            decay = (dim >= h * DECAY_TERMS) & (dim < (h + 1) * DECAY_TERMS)
            qx_ref[a, h, :LANES, :] = jnp.where(own, qt, jnp.zeros_like(qt))
            qx_ref[a, h, LANES:, :] = jnp.where(decay, 1.0, 0.0).astype(BF16)

    def reset():
        m_ref[...] = jnp.full_like(m_ref, -jnp.inf)
        acc_ref[...] = jnp.zeros_like(acc_ref)

    def scores(a, j, keep):
        kx = kx_ref[0, pl.ds(pl.multiple_of(j * tb, tb), tb), :]
        out = []
        for h in range(HEADS_PER_STEP):
            s = _dot(kx, qx_ref[a, h])
            out.append(s if keep is None else jnp.where(keep, s, -jnp.inf))
        return out

    def stash(s_new):
        for h in range(HEADS_PER_STEP):
            s_ref[h] = s_new[h]
            bm_ref[h] = jnp.max(s_new[h], axis=0, keepdims=True)

    def consume(j):
        start = pl.multiple_of(j * tb, tb)
        for h in range(HEADS_PER_STEP):
            m_prev = m_ref[h]
            m_new = jnp.maximum(m_prev, bm_ref[h])
            p = jnp.exp2(s_ref[h] - m_new)
            pv = _dot(vxt_ref[0, h * LANES:(h + 1) * LANES, pl.ds(start, tb)], p.astype(BF16))
            acc_ref[h] = jnp.exp2(m_prev - m_new) * acc_ref[h] + pv
            m_ref[h] = m_new

    def finalize(a):
        outs = []
        for h in range(HEADS_PER_STEP):
            acc = acc_ref[h]
            outs.append(acc[:FOX_HEAD_DIM, :] / acc[FOX_HEAD_DIM:FOX_HEAD_DIM + 1, :])
        o_ref[0, a * tb:(a + 1) * tb, :] = jnp.concatenate(outs, axis=0).T

    reset()
    stash(scores(0, step * FOX_Q_BLOCKS, causal))
    for a in range(FOX_Q_BLOCKS):
        qi = step * FOX_Q_BLOCKS + a
        first = first_ref[flat * FOX_Q_BLOCKS + a]

        def pipelined(t, a=a, qi=qi, first=first):
            s_next = scores(a, t, None)
            consume(jnp.where(t == first, qi, t - 1))
            stash(s_next)

        def unrolled(t, carry, pipelined=pipelined, first=first):
            for u in range(KV_UNROLL):
                pipelined(first + KV_UNROLL * t + u)
            return carry

        def single(t, carry, pipelined=pipelined):
            pipelined(t)
            return carry

        n_unrolled = _div_pow2(qi - first, KV_UNROLL)
        lax.fori_loop(0, n_unrolled, unrolled, 0)
        lax.fori_loop(first + n_unrolled * KV_UNROLL, qi, single, 0)
        last = jnp.where(qi > first, qi - 1, qi)
        if a + 1 < FOX_Q_BLOCKS:
            s_next = scores(a + 1, qi + 1, causal)
            consume(last)
            finalize(a)
            reset()
            stash(s_next)
        else:
            consume(last)
            finalize(a)


def _first_live_block(stats):
    st = stats[..., 0]
    B, nblk, _ = st.shape
    qn, kn, d_first, d_last = (st[:, :, i * N_FOX_HEADS:(i + 1) * N_FOX_HEADS] for i in range(N_TILE_STATS))
    bound = (qn[:, :, None] * (kn[:, None, :] + kn[:, :, None])
             + d_last[:, None, :] - d_first[:, :, None])
    dead = bound < -PRUNE_LOG2_GAP
    dead = dead.reshape(B, nblk, nblk, N_FOX_HEADS // HEADS_PER_STEP, HEADS_PER_STEP).all(axis=-1)
    below = jnp.arange(nblk)[None, :] < jnp.arange(nblk)[:, None]
    dead = jnp.logical_and(dead, below[None, :, :, None])
    first = jnp.min(jnp.where(dead, nblk, jnp.arange(nblk)[None, None, :, None]), axis=2)
    return first.astype(jnp.int32).transpose(0, 2, 1).reshape(-1)


def _fox_prompt(qt, kx, vxt, stats, *, tb):
    B, _, S = qt.shape
    tq = FOX_Q_BLOCKS * tb
    assert stats.shape[1] == S // tb and S % tq == 0
    return pl.pallas_call(
        functools.partial(_fox_prompt_kernel, tb=tb),
        grid_spec=pltpu.PrefetchScalarGridSpec(
            num_scalar_prefetch=1,
            grid=(B, FOX_WIDTH // LANES, S // tq),
            in_specs=[
                pl.BlockSpec((1, LANES, tq), lambda b, g, i, first: (b, g, i)),
                pl.BlockSpec((1, S, 2 * LANES), lambda b, g, i, first: (b, 0, g)),
                pl.BlockSpec((1, 2 * LANES, S), lambda b, g, i, first: (b, g, 0)),
            ],
            out_specs=pl.BlockSpec((1, tq, LANES), lambda b, g, i, first: (b, i, g)),
            scratch_shapes=[pltpu.VMEM((FOX_Q_BLOCKS, HEADS_PER_STEP, 2 * LANES, tb), BF16),
                            pltpu.VMEM((HEADS_PER_STEP, 1, tb), F32),
                            pltpu.VMEM((HEADS_PER_STEP, LANES, tb), F32),
                            pltpu.VMEM((HEADS_PER_STEP, tb, tb), F32),
                            pltpu.VMEM((HEADS_PER_STEP, 1, tb), F32)]),
        out_shape=jax.ShapeDtypeStruct((B, S, FOX_WIDTH), F32),
        compiler_params=_params("arbitrary", "arbitrary", "arbitrary"),
        name="fox_prompt",
    )(_first_live_block(stats), qt, kx, vxt)


def _fox_sample_kernel(q_ref, kn_ref, vn_ref, cn_ref, ck_ref, cv_ref, lft_ref, o_ref, s_ref, *, P, T):
    rows = N_FOX_HEADS * T
    q = q_ref[0]
    qt = jnp.concatenate([q] * N_FOX_HEADS, axis=0)
    r = lax.broadcasted_iota(jnp.int32, (rows, FOX_WIDTH), 0)
    c = lax.broadcasted_iota(jnp.int32, (rows, FOX_WIDTH), 1)
    wt = jnp.where(_div_pow2(r, T) == _div_pow2(c, FOX_HEAD_DIM), qt, jnp.zeros_like(qt))

    lane = lax.broadcasted_iota(jnp.int32, (N_FOX_HEADS, P), 1)
    cs = _lane_cumsum(lft_ref[0], lane, P)
    suffix = cs[:, P - 1:P] - cs

    def head_rows(x):
        return jnp.concatenate(
            [jnp.broadcast_to(x[h:h + 1, :], (T, x.shape[1])) for h in range(N_FOX_HEADS)], axis=0)

    kc_w = SAMPLE_KEY_CHUNK
    for k0 in range(0, P, kc_w):
        kct = ck_ref[0, :, k0:k0 + kc_w].astype(BF16)
        s_ref[:, k0:k0 + kc_w] = _dot(wt, kct) + head_rows(suffix[:, k0:k0 + kc_w])
    pad = jnp.zeros((LANES - T, FOX_WIDTH), BF16)
    sn = _dot_nt(wt, jnp.concatenate([kn_ref[0].astype(BF16), pad], axis=0))
    rr = lax.broadcasted_iota(jnp.int32, (rows, LANES), 0)
    cc = lax.broadcasted_iota(jnp.int32, (rows, LANES), 1)
    s_ref[:, P:P + LANES] = jnp.where(cc <= (rr & (T - 1)), sn - head_rows(cn_ref[0]), -jnp.inf)

    n_cols = P + LANES
    m = jnp.full((rows, 1), -jnp.inf, F32)
    for k0 in range(0, n_cols, kc_w):
        w = min(kc_w, n_cols - k0)
        m = jnp.maximum(m, jnp.max(s_ref[:, k0:k0 + w], axis=1, keepdims=True))
    l = jnp.zeros((rows, 1), F32)
    acc = jnp.zeros((rows, FOX_WIDTH), F32)
    for k0 in range(0, P, kc_w):
        p = jnp.exp(s_ref[:, k0:k0 + kc_w] - m)
        l = l + jnp.sum(p, axis=1, keepdims=True)
        acc = acc + _dot_nt(p.astype(BF16), cv_ref[0, :, k0:k0 + kc_w].astype(BF16))
    p = jnp.exp(s_ref[:, P:P + LANES] - m)
    l = l + jnp.sum(p, axis=1, keepdims=True)
    acc = acc + _dot(p.astype(BF16), jnp.concatenate([vn_ref[0].astype(BF16), pad], axis=0))
    on = acc / l
    c16 = lax.broadcasted_iota(jnp.int32, (T, FOX_WIDTH), 1)
    out = jnp.zeros((T, FOX_WIDTH), F32)
    for h in range(N_FOX_HEADS):
        out = out + jnp.where(_div_pow2(c16, FOX_HEAD_DIM) == h, on[h * T:(h + 1) * T, :], 0.0)
    o_ref[0] = out


def _fox_sample(q, kn, vn, cn_pad, cache_k, cache_v, cache_lft):
    Bd, T, _ = q.shape
    P = cache_k.shape[2]
    per_b = lambda shape: pl.BlockSpec((1,) + shape, lambda b: (b, 0, 0))
    return pl.pallas_call(
        functools.partial(_fox_sample_kernel, P=P, T=T),
        grid=(Bd,),
        in_specs=[per_b((T, FOX_WIDTH)), per_b((T, FOX_WIDTH)), per_b((T, FOX_WIDTH)),
                  per_b((N_FOX_HEADS, LANES)), per_b((FOX_WIDTH, P)), per_b((FOX_WIDTH, P)),
                  per_b((N_FOX_HEADS, P))],
        out_specs=per_b((T, FOX_WIDTH)),
        out_shape=jax.ShapeDtypeStruct((Bd, T, FOX_WIDTH), F32),
        scratch_shapes=[pltpu.VMEM((N_FOX_HEADS * T, P + LANES), F32)],
        compiler_params=_params("arbitrary"),
        name="fox_sample",
    )(q, kn, vn, cn_pad, cache_k, cache_v, cache_lft)


def _memory_kv_kernel(mem_ref, wk_ref, wv_ref, k_ref, v_ref):
    mb = mem_ref[...].astype(BF16)
    k_ref[...] = _dot(mb, wk_ref[...])
    v_ref[...] = _dot(mb, wv_ref[...])


def _memory_kv(mem2d, wk, wv):
    n = mem2d.shape[0]
    full = lambda a: pl.BlockSpec(a.shape, lambda i: (0, 0))
    out = jax.ShapeDtypeStruct((n, D_MODEL), F32)
    return pl.pallas_call(
        _memory_kv_kernel, grid=(1,),
        in_specs=[full(mem2d), full(wk), full(wv)],
        out_specs=[pl.BlockSpec((n, D_MODEL), lambda i: (0, 0))] * 2,
        out_shape=[out, out], compiler_params=_params("arbitrary"), name="memory_kv",
    )(mem2d, wk, wv)


def _merge_xattn_kernel(x_ref, att_ref, hg_ref, mk_ref, mv_ref, gfox_ref, wo_ref, ln1g_ref, ln1b_ref,
                        wmq_ref, wmo_ref, ln2g_ref, ln2b_ref, o_ref, *, alpha):
    parts = _row_parts(x_ref.shape[1])
    head = lambda hh: slice(hh * MEM_HEAD_DIM, (hh + 1) * MEM_HEAD_DIM)
    mk = mk_ref[0].astype(BF16)
    mv = mv_ref[0].astype(BF16)
    h = [jnp.concatenate([_rms_norm(att_ref[0, r, :], gfox_ref[...]).astype(BF16), hg_ref[0, r, :]], axis=1)
         for r in parts]
    mix = [_dot(hi, wo_ref[...]) for hi in h]
    x1 = [_layer_norm(alpha * x_ref[0, r, :] + mi, ln1g_ref[...], ln1b_ref[...]) for r, mi in zip(parts, mix)]
    qm = [_dot(xi.astype(BF16), wmq_ref[...]) for xi in x1]
    qm = [(qi * (MEM_HEAD_DIM ** -0.5)).astype(BF16) for qi in qm]
    s = [[_dot_nt(qi[:, head(hh)], mk[:, head(hh)]) for hh in range(N_MEM_HEADS)] for qi in qm]
    p = [[jnp.exp(sh - jnp.max(sh, axis=1, keepdims=True)) for sh in si] for si in s]
    o = [[_dot(ph.astype(BF16), mv[:, head(hh)]) / jnp.sum(ph, axis=1, keepdims=True)
          for hh, ph in enumerate(pi)] for pi in p]
    y = [_dot(jnp.concatenate(oi, axis=1).astype(BF16), wmo_ref[...]) for oi in o]
    for r, xi, yi in zip(parts, x1, y):
        o_ref[0, r, :] = _layer_norm(alpha * xi + yi, ln2g_ref[...], ln2b_ref[...])


def _merge_xattn(x, att, hg, mk, mv, gfox, wo, ln1g, ln1b, wmq, wmo, ln2g, ln2b, *, tm, alpha):
    G, R, D = x.shape
    row = lambda w: pl.BlockSpec((1, tm, w), lambda g, j: (g, j, 0))
    mem = pl.BlockSpec((1, N_MEM, D), lambda g, j: (g, 0, 0))
    full = lambda a: pl.BlockSpec(a.shape, lambda g, j: (0,) * a.ndim)
    return pl.pallas_call(
        functools.partial(_merge_xattn_kernel, alpha=alpha),
        grid=(G, R // tm),
        in_specs=[row(D), row(FOX_WIDTH), row(GMLP_WIDTH), mem, mem] +
                 [full(a) for a in (gfox, wo, ln1g, ln1b, wmq, wmo, ln2g, ln2b)],
        out_specs=row(D), out_shape=jax.ShapeDtypeStruct((G, R, D), F32),
        compiler_params=_params("arbitrary", "arbitrary"), name=f"merge_xattn_{tm}",
    )(x, att, hg, mk, mv, gfox, wo, ln1g, ln1b, wmq, wmo, ln2g, ln2b)


def _ffn_kernel(x_ref, wg_ref, wu_ref, wd_ref, g_ref, b_ref, o_ref, h_ref, *, alpha, d_ff):
    parts = _row_parts(x_ref.shape[1])
    xb = [x_ref[0, r, :].astype(BF16) for r in parts]
    for c0 in range(0, d_ff, FFN_COL_CHUNK):
        cols = slice(c0, min(c0 + FFN_COL_CHUNK, d_ff))
        for r, xi in zip(parts, xb):
            gate = _dot(xi, wg_ref[:, cols])
            up = _dot(xi, wu_ref[:, cols])
            h_ref[r, cols] = (jax.nn.silu(gate) * up).astype(BF16)
    y = [_dot(h_ref[r, :], wd_ref[...]) for r in parts]
    for r, yi in zip(parts, y):
        o_ref[0, r, :] = _layer_norm(alpha * x_ref[0, r, :] + yi, g_ref[...], b_ref[...])


def _ffn(x, wg, wu, wd, g, b, *, tm, alpha):
    G, R, D = x.shape
    d_ff = wg.shape[1]
    assert d_ff % LANES == 0
    row = pl.BlockSpec((1, tm, D), lambda gi, j: (gi, j, 0))
    full = lambda a: pl.BlockSpec(a.shape, lambda gi, j: (0,) * a.ndim)
    return pl.pallas_call(
        functools.partial(_ffn_kernel, alpha=alpha, d_ff=d_ff),
        grid=(G, R // tm),
        in_specs=[row] + [full(a) for a in (wg, wu, wd, g, b)],
        out_specs=row, out_shape=jax.ShapeDtypeStruct((G, R, D), F32),
        scratch_shapes=[pltpu.VMEM((tm, d_ff), BF16)],
        compiler_params=_params("arbitrary", "arbitrary"), name=f"ffn_{tm}",
    )(x, wg, wu, wd, g, b)


def _row(a):
    return a.reshape(1, -1)


def kernel(x_prompt, x_sample, cache_fox_k, cache_fox_v, cache_fox_logf, cache_mem_k, cache_mem_v, mem_prompt, w_in, b_f, g_fox_out, g_gmlp_out, sgu_ln_g, sgu_ln_b, w_s, b_s, w_o, ln1_g, ln1_b, w_mq, w_mk, w_mv, w_mo, ln2_g, ln2_b, w_gate, w_up, w_down, ln3_g, ln3_b):
    depth = w_in.shape[0]
    B, S, D = x_prompt.shape
    Bd, T, _ = x_sample.shape
    P = cache_fox_k.shape[2]
    alpha = (2.0 * depth) ** 0.25
    assert D == D_MODEL and S % FOX_BLOCK == 0 and S % PROMPT_ROW_TILE == 0
    assert PROMPT_ROW_TILE % GMLP_CHUNK == 0 and (T & (T - 1)) == 0 and N_FOX_HEADS * T == LANES
    assert P % SAMPLE_KEY_CHUNK == 0 and S % TAIL_ROW_TILE == 0 and S % FFN_ROW_TILE == 0

    yp, ys = x_prompt, x_sample
    outs = [[] for _ in range(9)]
    for l in range(depth):
        w = w_in[l].astype(BF16)
        wq, wk, wv = (w[:, i * FOX_WIDTH:(i + 1) * FOX_WIDTH] for i in range(3))
        f0 = 3 * FOX_WIDTH
        wf = jnp.pad(w[:, f0:f0 + N_FOX_HEADS], ((0, 0), (0, LANES - N_FOX_HEADS)))
        bfp = jnp.pad(_row(b_f[l]), ((0, 0), (0, LANES - N_FOX_HEADS)))
        wg = w[:, f0 + N_FOX_HEADS:]
        lng, lnb, gout = _row(sgu_ln_g[l]), _row(sgu_ln_b[l]), _row(g_gmlp_out[l])
        tail_w = (_row(g_fox_out[l]), w_o[l].astype(BF16), _row(ln1_g[l]), _row(ln1_b[l]),
                  w_mq[l].astype(BF16), w_mo[l].astype(BF16), _row(ln2_g[l]), _row(ln2_b[l]))
        ffn_w = (w_gate[l].astype(BF16), w_up[l].astype(BF16), w_down[l].astype(BF16),
                 _row(ln3_g[l]), _row(ln3_b[l]))

        def mix_params(n):
            bias = jnp.repeat(b_s[l][:, :n].T, GMLP_GROUP_DIM, axis=1)
            return w_s[l][:, :n, :n], bias

        ws_p, bs_p = mix_params(GMLP_CHUNK)
        k, v, lf, hg, qt, kx, vxt, stats = _in_proj_mix(
            yp, wq, wk, wv, wf, bfp, wg, lng, lnb, ws_p, bs_p, gout,
            tm=PROMPT_ROW_TILE, chunk=GMLP_CHUNK, seg=None, prompt=True)
        att = _fox_prompt(qt, kx, vxt, stats, tb=FOX_BLOCK)
        mk, mv = _memory_kv(mem_prompt.reshape(B * N_MEM, D), w_mk[l].astype(BF16), w_mv[l].astype(BF16))
        mk, mv = mk.reshape(B, N_MEM, D), mv.reshape(B, N_MEM, D)
        x2 = _merge_xattn(yp, att, hg, mk, mv, *tail_w, tm=TAIL_ROW_TILE, alpha=alpha)
        yp = _ffn(x2, *ffn_w, tm=FFN_ROW_TILE, alpha=alpha)
        heads_last = lambda a: a.reshape(B, N_FOX_HEADS, FOX_HEAD_DIM, S).transpose(0, 3, 1, 2)
        outs[0].append(heads_last(k))
        outs[1].append(heads_last(v))
        outs[2].append(lf.transpose(0, 2, 1))
        outs[3].append(mk.reshape(B, N_MEM, N_MEM_HEADS, MEM_HEAD_DIM))
        outs[4].append(mv.reshape(B, N_MEM, N_MEM_HEADS, MEM_HEAD_DIM))

        ws_s, bs_s = mix_params(T)
        k, v, lf, hg, q, ct, gv = _in_proj_mix(
            ys.reshape(1, Bd * T, D), wq, wk, wv, wf, bfp, wg, lng, lnb, ws_s, bs_s, gout,
            tm=Bd * T, chunk=T, seg=T, prompt=False)
        per_b = lambda a: a.reshape(Bd, T, a.shape[-1])
        cn = ct.reshape(N_FOX_HEADS, Bd, T).transpose(1, 0, 2)
        cn = jnp.pad(cn, ((0, 0), (0, 0), (0, LANES - T)))
        dims_first = lambda a: a.transpose(0, 2, 3, 1).reshape(Bd, FOX_WIDTH, P)
        att = _fox_sample(per_b(q), per_b(k), per_b(v), cn,
                          dims_first(cache_fox_k[l]), dims_first(cache_fox_v[l]),
                          jnp.swapaxes(cache_fox_logf[l], 1, 2))
        x2 = _merge_xattn(ys, att, per_b(hg), cache_mem_k[l].reshape(Bd, N_MEM, D),
                          cache_mem_v[l].reshape(Bd, N_MEM, D), *tail_w, tm=T, alpha=alpha)
        ys = _ffn(x2.reshape(1, Bd * T, D), *ffn_w, tm=Bd * T, alpha=alpha).reshape(Bd, T, D)
        outs[5].append(k.reshape(Bd, T, N_FOX_HEADS, FOX_HEAD_DIM))
        outs[6].append(v.reshape(Bd, T, N_FOX_HEADS, FOX_HEAD_DIM))
        outs[7].append(per_b(lf))
        outs[8].append(per_b(gv))
    return (yp, ys) + tuple(jnp.stack(o) for o in outs)
```

```python
import functools

import jax
import jax.numpy as jnp
from jax import lax
from jax.experimental import pallas as pl
from jax.experimental.pallas import tpu as pltpu

F32 = jnp.float32
BF16 = jnp.bfloat16

LANES = 128
D_MODEL = 1024
STREAM_CHUNK = 64
N_FOX_HEADS = 8
FOX_HEAD_DIM = 64
FOX_WIDTH = N_FOX_HEADS * FOX_HEAD_DIM
HEADS_PER_STEP = LANES // FOX_HEAD_DIM
GMLP_WIDTH = D_MODEL - FOX_WIDTH
GMLP_GROUPS = 4
GMLP_GROUP_DIM = GMLP_WIDTH // GMLP_GROUPS
GMLP_CHUNK = 128
N_MEM = 256
N_MEM_HEADS = 4
MEM_HEAD_DIM = D_MODEL // N_MEM_HEADS
EPS = 1e-5
LOG2E = 1.4426950408889634
DECAY_TERMS = 3
DECAY_ROWS = 16
assert HEADS_PER_STEP * DECAY_TERMS <= DECAY_ROWS
N_TILE_STATS = 4
PRUNE_LOG2_GAP = 160.0
VMEM_LIMIT = 56 * 1024 * 1024

PROMPT_ROW_TILE = 512
FOX_BLOCK = 512
KV_UNROLL = 4
FOX_Q_BLOCKS = 4
TAIL_ROW_TILE = 1024
TAIL_PART_ROWS = 256
FFN_ROW_TILE = 1024
FFN_COL_CHUNK = 512
SAMPLE_KEY_CHUNK = 512


def _dot(a, b):
    return jnp.dot(a, b, preferred_element_type=F32)


def _dot_nt(a, b):
    return lax.dot_general(a, b, (((1,), (1,)), ((), ())), preferred_element_type=F32)


def _layer_norm(x, g, b):
    mu = jnp.mean(x, axis=-1, keepdims=True)
    xc = x - mu
    var = jnp.mean(xc * xc, axis=-1, keepdims=True)
    return xc * lax.rsqrt(var + EPS) * g + b


def _rms_norm(x, g):
    return x * lax.rsqrt(jnp.mean(x * x, axis=-1, keepdims=True) + EPS) * g


def _lane_cumsum(x, pos, length):
    shift = 1
    while shift < length:
        x = x + jnp.where(pos >= shift, pltpu.roll(x, shift, axis=1), 0.0)
        shift *= 2
    return x


def _div_pow2(x, n):
    assert n & (n - 1) == 0
    return x >> (n.bit_length() - 1)


def _row_parts(tm):
    n = tm // TAIL_PART_ROWS if tm % TAIL_PART_ROWS == 0 else 1
    return [slice(i * tm // n, (i + 1) * tm // n) for i in range(n)]


def _params(*semantics):
    return pltpu.CompilerParams(dimension_semantics=semantics, vmem_limit_bytes=VMEM_LIMIT)


def _in_proj_kernel(*refs, tm, chunk, seg, prompt):
    (x_ref, wq_ref, wk_ref, wv_ref, wf_ref, bf_ref, wg_ref, lng_ref, lnb_ref,
     ws_ref, bs_ref, gout_ref) = refs[:12]
    if prompt:
        k_ref, v_ref, lf_ref, hg_ref, qt_ref, kx_ref, vxt_ref, st_ref, carry_ref = refs[12:]
    else:
        k_ref, v_ref, lf_ref, hg_ref, q_ref, ct_ref, gv_ref = refs[12:]

    if prompt:
        @pl.when(pl.program_id(1) == 0)
        def _():
            carry_ref[...] = jnp.zeros_like(carry_ref)

    xb = x_ref[0].astype(BF16)
    zf = _dot(xb, wf_ref[...]) + bf_ref[...]
    zg = _dot(xb, wg_ref[...])
    zq = _dot(xb, wq_ref[...])
    if prompt:
        qtb = (zq * (FOX_HEAD_DIM ** -0.5 * LOG2E)).T.astype(BF16)
        qt_ref[0] = qtb
    else:
        q_ref[0] = (zq * (FOX_HEAD_DIM ** -0.5)).astype(BF16)
    zk = _dot(xb, wk_ref[...])
    zv = _dot(xb, wv_ref[...])
    lf = jax.nn.log_sigmoid(zf)
    lft = lf.T[:N_FOX_HEADS, :]
    if prompt:
        zvt = zv.T
        zkt = zk.T
        k_ref[0] = zkt
        v_ref[0] = zvt
        lf_ref[0] = lft
    else:
        k_ref[0] = zk
        v_ref[0] = zv
        lf_ref[0] = lf[:, :N_FOX_HEADS]
    lane = lax.broadcasted_iota(jnp.int32, (N_FOX_HEADS, tm), 1)
    if prompt:
        c = _lane_cumsum(lft, lane, tm) + carry_ref[...]
        carry_ref[...] = jnp.broadcast_to(c[:, tm - 1:tm], (N_FOX_HEADS, tm))

        decay = c * (-LOG2E)

        def max_head_norms(xt):
            sq = xt * xt
            hid = lax.broadcasted_iota(jnp.int32, (N_FOX_HEADS, LANES), 0)
            out = jnp.zeros((N_FOX_HEADS, LANES), F32)
            for hd in range(N_FOX_HEADS):
                n2 = jnp.sum(sq[hd * FOX_HEAD_DIM:(hd + 1) * FOX_HEAD_DIM, :], axis=0, keepdims=True)
                out = jnp.where(hid == hd, jnp.max(jnp.sqrt(n2), axis=1, keepdims=True), out)
            return out

        st_ref[0, 0, 0 * N_FOX_HEADS:1 * N_FOX_HEADS, :] = max_head_norms(qtb.astype(F32))
        st_ref[0, 0, 1 * N_FOX_HEADS:2 * N_FOX_HEADS, :] = max_head_norms(zkt.astype(BF16).astype(F32))
        st_ref[0, 0, 2 * N_FOX_HEADS:3 * N_FOX_HEADS, :] = jnp.broadcast_to(decay[:, 0:1], (N_FOX_HEADS, LANES))
        st_ref[0, 0, 3 * N_FOX_HEADS:4 * N_FOX_HEADS, :] = jnp.broadcast_to(decay[:, tm - 1:tm], (N_FOX_HEADS, LANES))

        rest = decay
        terms = []
        for _ in range(DECAY_TERMS):
            t = rest.astype(BF16).astype(F32)
            terms.append(t)
            rest = rest - t
        rid = lax.broadcasted_iota(jnp.int32, (DECAY_ROWS, tm), 0)
        for g in range(FOX_WIDTH // LANES):
            dec = jnp.zeros((DECAY_ROWS, tm), F32)
            for hh in range(HEADS_PER_STEP):
                hd = g * HEADS_PER_STEP + hh
                for ti, t in enumerate(terms):
                    dec = jnp.where(rid == hh * DECAY_TERMS + ti, t[hd:hd + 1, :], dec)
            dec = jnp.concatenate([dec, jnp.zeros((LANES - DECAY_ROWS, tm), F32)], axis=0)
            kx_ref[0, :, 2 * g * LANES:(2 * g + 1) * LANES] = zk[:, g * LANES:(g + 1) * LANES].astype(BF16)
            kx_ref[0, :, (2 * g + 1) * LANES:(2 * g + 2) * LANES] = dec.T.astype(BF16)
        for hd in range(N_FOX_HEADS):
            vxt_ref[0, hd * LANES:hd * LANES + FOX_HEAD_DIM, :] = (
                zvt[hd * FOX_HEAD_DIM:(hd + 1) * FOX_HEAD_DIM, :].astype(BF16))
            vxt_ref[0, hd * LANES + FOX_HEAD_DIM:(hd + 1) * LANES, :] = jnp.ones((LANES - FOX_HEAD_DIM, tm), BF16)
    else:
        ct_ref[0] = _lane_cumsum(lft, lane & (seg - 1), seg)

    z = jax.nn.gelu(zg)
    u = z[:, :GMLP_WIDTH]
    vn = _layer_norm(z[:, GMLP_WIDTH:], lng_ref[...], lnb_ref[...])
    if not prompt:
        gv_ref[0] = vn
    vnb = vn.astype(BF16)
    ri = lax.broadcasted_iota(jnp.int32, (chunk, chunk), 0)
    ci = lax.broadcasted_iota(jnp.int32, (chunk, chunk), 1)
    causal = _div_pow2(ci, STREAM_CHUNK) <= _div_pow2(ri, STREAM_CHUNK)
    wm = [jnp.where(causal, ws_ref[g], 0.0).astype(BF16) for g in range(GMLP_GROUPS)]
    for r0 in range(0, tm, chunk):
        mixed = jnp.concatenate(
            [_dot(wm[g], vnb[r0:r0 + chunk, g * GMLP_GROUP_DIM:(g + 1) * GMLP_GROUP_DIM])
             for g in range(GMLP_GROUPS)], axis=1) + bs_ref[...]
        gm = u[r0:r0 + chunk, :] * mixed
        hg_ref[0, r0:r0 + chunk, :] = _rms_norm(gm, gout_ref[...]).astype(BF16)


def _in_proj_mix(x, wq, wk, wv, wf, bfp, wg, lng, lnb, ws, bs_full, gout, *, tm, chunk, seg, prompt):
    G, R, D = x.shape
    grid = (G, R // tm)
    row = lambda w: pl.BlockSpec((1, tm, w), lambda g, j: (g, j, 0))
    full = lambda a: pl.BlockSpec(a.shape, lambda g, j: (0,) * a.ndim)
    in_specs = [row(D)] + [full(a) for a in (wq, wk, wv, wf, bfp, wg, lng, lnb, ws, bs_full, gout)]
    col = lambda h: pl.BlockSpec((1, h, tm), lambda g, j: (g, 0, j))
    wide = (lambda w: (G, w, R)) if prompt else (lambda w: (G, R, w))
    spec = col if prompt else row
    out_shape = [
        jax.ShapeDtypeStruct(wide(FOX_WIDTH), F32),
        jax.ShapeDtypeStruct(wide(FOX_WIDTH), F32),
        jax.ShapeDtypeStruct(wide(N_FOX_HEADS), F32),
        jax.ShapeDtypeStruct((G, R, GMLP_WIDTH), BF16),
    ]
    out_specs = [spec(FOX_WIDTH), spec(FOX_WIDTH), spec(N_FOX_HEADS), row(GMLP_WIDTH)]
    scratch = []
    if prompt:
        out_shape += [
            jax.ShapeDtypeStruct((G, FOX_WIDTH, R), BF16),
            jax.ShapeDtypeStruct((G, R, 2 * FOX_WIDTH), BF16),
            jax.ShapeDtypeStruct((G, N_FOX_HEADS * LANES, R), BF16),
            jax.ShapeDtypeStruct((G, R // tm, N_TILE_STATS * N_FOX_HEADS, LANES), F32)]
        out_specs += [col(FOX_WIDTH), row(2 * FOX_WIDTH), col(N_FOX_HEADS * LANES),
                      pl.BlockSpec((1, 1, N_TILE_STATS * N_FOX_HEADS, LANES), lambda g, j: (g, j, 0, 0))]
        scratch = [pltpu.VMEM((N_FOX_HEADS, tm), F32)]
    else:
        out_shape += [jax.ShapeDtypeStruct((G, R, FOX_WIDTH), BF16),
                      jax.ShapeDtypeStruct((G, N_FOX_HEADS, R), F32),
                      jax.ShapeDtypeStruct((G, R, GMLP_WIDTH), F32)]
        out_specs += [row(FOX_WIDTH), col(N_FOX_HEADS), row(GMLP_WIDTH)]
    return pl.pallas_call(
        functools.partial(_in_proj_kernel, tm=tm, chunk=chunk, seg=seg, prompt=prompt),
        grid=grid, in_specs=in_specs, out_specs=out_specs, out_shape=out_shape,
        scratch_shapes=scratch, compiler_params=_params("arbitrary", "arbitrary"),
        name="in_proj_mix_prompt" if prompt else "in_proj_mix_sample",
    )(x, wq, wk, wv, wf, bfp, wg, lng, lnb, ws, bs_full, gout)


def _fox_prompt_kernel(first_ref, qt_ref, kx_ref, vxt_ref, o_ref, qx_ref, m_ref, acc_ref, s_ref, bm_ref, *, tb):
    step = pl.program_id(2)
    flat = (pl.program_id(0) * pl.num_programs(1) + pl.program_id(1)) * pl.num_programs(2) + step
    dim = lax.broadcasted_iota(jnp.int32, (LANES, tb), 0)
    key = lax.broadcasted_iota(jnp.int32, (tb, tb), 0)
    qry = lax.broadcasted_iota(jnp.int32, (tb, tb), 1)
    causal = key <= qry
    for a in range(FOX_Q_BLOCKS):
        qt = qt_ref[0, :, a * tb:(a + 1) * tb]
        for h in range(HEADS_PER_STEP):
            own = _div_pow2(dim, FOX_HEAD_DIM) == h
            decay = (dim >= h * DECAY_TERMS) & (dim < (h + 1) * DECAY_TERMS)
            qx_ref[a, h, :LANES, :] = jnp.where(own, qt, jnp.zeros_like(qt))
            qx_ref[a, h, LANES:, :] = jnp.where(decay, 1.0, 0.0).astype(BF16)

    def reset():
        m_ref[...] = jnp.full_like(m_ref, -jnp.inf)
        acc_ref[...] = jnp.zeros_like(acc_ref)

    def scores(a, j, keep):
        kx = kx_ref[0, pl.ds(pl.multiple_of(j * tb, tb), tb), :]
        out = []
        for h in range(HEADS_PER_STEP):
            s = _dot(kx, qx_ref[a, h])
            out.append(s if keep is None else jnp.where(keep, s, -jnp.inf))
        return out

    def stash(s_new):
        for h in range(HEADS_PER_STEP):
            s_ref[h] = s_new[h]
            bm_ref[h] = jnp.max(s_new[h], axis=0, keepdims=True)

    def consume(j):
        start = pl.multiple_of(j * tb, tb)
        for h in range(HEADS_PER_STEP):
            m_prev = m_ref[h]
            m_new = jnp.maximum(m_prev, bm_ref[h])
            p = jnp.exp2(s_ref[h] - m_new)
            pv = _dot(vxt_ref[0, h * LANES:(h + 1) * LANES, pl.ds(start, tb)], p.astype(BF16))
            acc_ref[h] = jnp.exp2(m_prev - m_new) * acc_ref[h] + pv
            m_ref[h] = m_new

    def finalize(a):
        outs = []
        for h in range(HEADS_PER_STEP):
            acc = acc_ref[h]
            outs.append(acc[:FOX_HEAD_DIM, :] / acc[FOX_HEAD_DIM:FOX_HEAD_DIM + 1, :])
        o_ref[0, a * tb:(a + 1) * tb, :] = jnp.concatenate(outs, axis=0).T

    reset()
    stash(scores(0, step * FOX_Q_BLOCKS, causal))
    for a in range(FOX_Q_BLOCKS):
        qi = step * FOX_Q_BLOCKS + a
        first = first_ref[flat * FOX_Q_BLOCKS + a]

        def pipelined(t, a=a, qi=qi, first=first):
            s_next = scores(a, t, None)
            consume(jnp.where(t == first, qi, t - 1))
            stash(s_next)

        def unrolled(t, carry, pipelined=pipelined, first=first):
            for u in range(KV_UNROLL):
                pipelined(first + KV_UNROLL * t + u)
            return carry

        def single(t, carry, pipelined=pipelined):
            pipelined(t)
            return carry

        n_unrolled = _div_pow2(qi - first, KV_UNROLL)
        lax.fori_loop(0, n_unrolled, unrolled, 0)
        lax.fori_loop(first + n_unrolled * KV_UNROLL, qi, single, 0)
        last = jnp.where(qi > first, qi - 1, qi)
        if a + 1 < FOX_Q_BLOCKS:
            s_next = scores(a + 1, qi + 1, causal)
            consume(last)
            finalize(a)
            reset()
            stash(s_next)
        else:
            consume(last)
            finalize(a)


def _first_live_block(stats):
    st = stats[..., 0]
    B, nblk, _ = st.shape
    qn, kn, d_first, d_last = (st[:, :, i * N_FOX_HEADS:(i + 1) * N_FOX_HEADS] for i in range(N_TILE_STATS))
    bound = (qn[:, :, None] * (kn[:, None, :] + kn[:, :, None])
             + d_last[:, None, :] - d_first[:, :, None])
    dead = bound < -PRUNE_LOG2_GAP
    dead = dead.reshape(B, nblk, nblk, N_FOX_HEADS // HEADS_PER_STEP, HEADS_PER_STEP).all(axis=-1)
    below = jnp.arange(nblk)[None, :] < jnp.arange(nblk)[:, None]
    dead = jnp.logical_and(dead, below[None, :, :, None])
    first = jnp.min(jnp.where(dead, nblk, jnp.arange(nblk)[None, None, :, None]), axis=2)
    return first.astype(jnp.int32).transpose(0, 2, 1).reshape(-1)


def _fox_prompt(qt, kx, vxt, stats, *, tb):
    B, _, S = qt.shape
    tq = FOX_Q_BLOCKS * tb
    assert stats.shape[1] == S // tb and S % tq == 0
    return pl.pallas_call(
        functools.partial(_fox_prompt_kernel, tb=tb),
        grid_spec=pltpu.PrefetchScalarGridSpec(
            num_scalar_prefetch=1,
            grid=(B, FOX_WIDTH // LANES, S // tq),
            in_specs=[
                pl.BlockSpec((1, LANES, tq), lambda b, g, i, first: (b, g, i)),
                pl.BlockSpec((1, S, 2 * LANES), lambda b, g, i, first: (b, 0, g)),
                pl.BlockSpec((1, 2 * LANES, S), lambda b, g, i, first: (b, g, 0)),
            ],
            out_specs=pl.BlockSpec((1, tq, LANES), lambda b, g, i, first: (b, i, g)),
            scratch_shapes=[pltpu.VMEM((FOX_Q_BLOCKS, HEADS_PER_STEP, 2 * LANES, tb), BF16),
                            pltpu.VMEM((HEADS_PER_STEP, 1, tb), F32),
                            pltpu.VMEM((HEADS_PER_STEP, LANES, tb), F32),
                            pltpu.VMEM((HEADS_PER_STEP, tb, tb), F32),
                            pltpu.VMEM((HEADS_PER_STEP, 1, tb), F32)]),
        out_shape=jax.ShapeDtypeStruct((B, S, FOX_WIDTH), F32),
        compiler_params=_params("arbitrary", "arbitrary", "arbitrary"),
        name="fox_prompt",
    )(_first_live_block(stats), qt, kx, vxt)


def _fox_sample_kernel(q_ref, kn_ref, vn_ref, cn_ref, ck_ref, cv_ref, lft_ref, o_ref, s_ref, *, P, T):
    rows = N_FOX_HEADS * T
    q = q_ref[0]
    qt = jnp.concatenate([q] * N_FOX_HEADS, axis=0)
    r = lax.broadcasted_iota(jnp.int32, (rows, FOX_WIDTH), 0)
    c = lax.broadcasted_iota(jnp.int32, (rows, FOX_WIDTH), 1)
    wt = jnp.where(_div_pow2(r, T) == _div_pow2(c, FOX_HEAD_DIM), qt, jnp.zeros_like(qt))

    lane = lax.broadcasted_iota(jnp.int32, (N_FOX_HEADS, P), 1)
    cs = _lane_cumsum(lft_ref[0], lane, P)
    suffix = cs[:, P - 1:P] - cs

    def head_rows(x):
        return jnp.concatenate(
            [jnp.broadcast_to(x[h:h + 1, :], (T, x.shape[1])) for h in range(N_FOX_HEADS)], axis=0)

    kc_w = SAMPLE_KEY_CHUNK
    for k0 in range(0, P, kc_w):
        kct = ck_ref[0, :, k0:k0 + kc_w].astype(BF16)
        s_ref[:, k0:k0 + kc_w] = _dot(wt, kct) + head_rows(suffix[:, k0:k0 + kc_w])
    pad = jnp.zeros((LANES - T, FOX_WIDTH), BF16)
    sn = _dot_nt(wt, jnp.concatenate([kn_ref[0].astype(BF16), pad], axis=0))
    rr = lax.broadcasted_iota(jnp.int32, (rows, LANES), 0)
    cc = lax.broadcasted_iota(jnp.int32, (rows, LANES), 1)
    s_ref[:, P:P + LANES] = jnp.where(cc <= (rr & (T - 1)), sn - head_rows(cn_ref[0]), -jnp.inf)

    n_cols = P + LANES
    m = jnp.full((rows, 1), -jnp.inf, F32)
    for k0 in range(0, n_cols, kc_w):
        w = min(kc_w, n_cols - k0)
        m = jnp.maximum(m, jnp.max(s_ref[:, k0:k0 + w], axis=1, keepdims=True))
    l = jnp.zeros((rows, 1), F32)
    acc = jnp.zeros((rows, FOX_WIDTH), F32)
    for k0 in range(0, P, kc_w):
        p = jnp.exp(s_ref[:, k0:k0 + kc_w] - m)
        l = l + jnp.sum(p, axis=1, keepdims=True)
        acc = acc + _dot_nt(p.astype(BF16), cv_ref[0, :, k0:k0 + kc_w].astype(BF16))
    p = jnp.exp(s_ref[:, P:P + LANES] - m)
    l = l + jnp.sum(p, axis=1, keepdims=True)
    acc = acc + _dot(p.astype(BF16), jnp.concatenate([vn_ref[0].astype(BF16), pad], axis=0))
    on = acc / l
    c16 = lax.broadcasted_iota(jnp.int32, (T, FOX_WIDTH), 1)
    out = jnp.zeros((T, FOX_WIDTH), F32)
    for h in range(N_FOX_HEADS):
        out = out + jnp.where(_div_pow2(c16, FOX_HEAD_DIM) == h, on[h * T:(h + 1) * T, :], 0.0)
    o_ref[0] = out


def _fox_sample(q, kn, vn, cn_pad, cache_k, cache_v, cache_lft):
    Bd, T, _ = q.shape
    P = cache_k.shape[2]
    per_b = lambda shape: pl.BlockSpec((1,) + shape, lambda b: (b, 0, 0))
    return pl.pallas_call(
        functools.partial(_fox_sample_kernel, P=P, T=T),
        grid=(Bd,),
        in_specs=[per_b((T, FOX_WIDTH)), per_b((T, FOX_WIDTH)), per_b((T, FOX_WIDTH)),
                  per_b((N_FOX_HEADS, LANES)), per_b((FOX_WIDTH, P)), per_b((FOX_WIDTH, P)),
                  per_b((N_FOX_HEADS, P))],
        out_specs=per_b((T, FOX_WIDTH)),
        out_shape=jax.ShapeDtypeStruct((Bd, T, FOX_WIDTH), F32),
        scratch_shapes=[pltpu.VMEM((N_FOX_HEADS * T, P + LANES), F32)],
        compiler_params=_params("arbitrary"),
        name="fox_sample",
    )(q, kn, vn, cn_pad, cache_k, cache_v, cache_lft)


def _memory_kv_kernel(mem_ref, wk_ref, wv_ref, k_ref, v_ref):
    mb = mem_ref[...].astype(BF16)
    k_ref[...] = _dot(mb, wk_ref[...])
    v_ref[...] = _dot(mb, wv_ref[...])


def _memory_kv(mem2d, wk, wv):
    n = mem2d.shape[0]
    full = lambda a: pl.BlockSpec(a.shape, lambda i: (0, 0))
    out = jax.ShapeDtypeStruct((n, D_MODEL), F32)
    return pl.pallas_call(
        _memory_kv_kernel, grid=(1,),
        in_specs=[full(mem2d), full(wk), full(wv)],
        out_specs=[pl.BlockSpec((n, D_MODEL), lambda i: (0, 0))] * 2,
        out_shape=[out, out], compiler_params=_params("arbitrary"), name="memory_kv",
    )(mem2d, wk, wv)


def _merge_xattn_kernel(x_ref, att_ref, hg_ref, mk_ref, mv_ref, gfox_ref, wo_ref, ln1g_ref, ln1b_ref,
                        wmq_ref, wmo_ref, ln2g_ref, ln2b_ref, o_ref, *, alpha):
    parts = _row_parts(x_ref.shape[1])
    head = lambda hh: slice(hh * MEM_HEAD_DIM, (hh + 1) * MEM_HEAD_DIM)
    mk = mk_ref[0].astype(BF16)
    mv = mv_ref[0].astype(BF16)
    h = [jnp.concatenate([_rms_norm(att_ref[0, r, :], gfox_ref[...]).astype(BF16), hg_ref[0, r, :]], axis=1)
         for r in parts]
    mix = [_dot(hi, wo_ref[...]) for hi in h]
    x1 = [_layer_norm(alpha * x_ref[0, r, :] + mi, ln1g_ref[...], ln1b_ref[...]) for r, mi in zip(parts, mix)]
    qm = [_dot(xi.astype(BF16), wmq_ref[...]) for xi in x1]
    qm = [(qi * (MEM_HEAD_DIM ** -0.5)).astype(BF16) for qi in qm]
    s = [[_dot_nt(qi[:, head(hh)], mk[:, head(hh)]) for hh in range(N_MEM_HEADS)] for qi in qm]
    p = [[jnp.exp(sh - jnp.max(sh, axis=1, keepdims=True)) for sh in si] for si in s]
    o = [[_dot(ph.astype(BF16), mv[:, head(hh)]) / jnp.sum(ph, axis=1, keepdims=True)
          for hh, ph in enumerate(pi)] for pi in p]
    y = [_dot(jnp.concatenate(oi, axis=1).astype(BF16), wmo_ref[...]) for oi in o]
    for r, xi, yi in zip(parts, x1, y):
        o_ref[0, r, :] = _layer_norm(alpha * xi + yi, ln2g_ref[...], ln2b_ref[...])


def _merge_xattn(x, att, hg, mk, mv, gfox, wo, ln1g, ln1b, wmq, wmo, ln2g, ln2b, *, tm, alpha):
    G, R, D = x.shape
    row = lambda w: pl.BlockSpec((1, tm, w), lambda g, j: (g, j, 0))
    mem = pl.BlockSpec((1, N_MEM, D), lambda g, j: (g, 0, 0))
    full = lambda a: pl.BlockSpec(a.shape, lambda g, j: (0,) * a.ndim)
    return pl.pallas_call(
        functools.partial(_merge_xattn_kernel, alpha=alpha),
        grid=(G, R // tm),
        in_specs=[row(D), row(FOX_WIDTH), row(GMLP_WIDTH), mem, mem] +
                 [full(a) for a in (gfox, wo, ln1g, ln1b, wmq, wmo, ln2g, ln2b)],
        out_specs=row(D), out_shape=jax.ShapeDtypeStruct((G, R, D), F32),
        compiler_params=_params("arbitrary", "arbitrary"), name=f"merge_xattn_{tm}",
    )(x, att, hg, mk, mv, gfox, wo, ln1g, ln1b, wmq, wmo, ln2g, ln2b)


def _ffn_kernel(x_ref, wg_ref, wu_ref, wd_ref, g_ref, b_ref, o_ref, h_ref, *, alpha, d_ff):
    parts = _row_parts(x_ref.shape[1])
    xb = [x_ref[0, r, :].astype(BF16) for r in parts]
    for c0 in range(0, d_ff, FFN_COL_CHUNK):
        cols = slice(c0, min(c0 + FFN_COL_CHUNK, d_ff))
        for r, xi in zip(parts, xb):
            gate = _dot(xi, wg_ref[:, cols])
            up = _dot(xi, wu_ref[:, cols])
            h_ref[r, cols] = (jax.nn.silu(gate) * up).astype(BF16)
    y = [_dot(h_ref[r, :], wd_ref[...]) for r in parts]
    for r, yi in zip(parts, y):
        o_ref[0, r, :] = _layer_norm(alpha * x_ref[0, r, :] + yi, g_ref[...], b_ref[...])


def _ffn(x, wg, wu, wd, g, b, *, tm, alpha):
    G, R, D = x.shape
    d_ff = wg.shape[1]
    assert d_ff % LANES == 0
    row = pl.BlockSpec((1, tm, D), lambda gi, j: (gi, j, 0))
    full = lambda a: pl.BlockSpec(a.shape, lambda gi, j: (0,) * a.ndim, pipeline_mode=pl.Buffered(1))
    return pl.pallas_call(
        functools.partial(_ffn_kernel, alpha=alpha, d_ff=d_ff),
        grid=(G, R // tm),
        in_specs=[row] + [full(a) for a in (wg, wu, wd, g, b)],
        out_specs=row, out_shape=jax.ShapeDtypeStruct((G, R, D), F32),
        scratch_shapes=[pltpu.VMEM((tm, d_ff), BF16)],
        compiler_params=_params("arbitrary", "arbitrary"), name=f"ffn_{tm}",
    )(x, wg, wu, wd, g, b)


def _row(a):
    return a.reshape(1, -1)


def kernel(x_prompt, x_sample, cache_fox_k, cache_fox_v, cache_fox_logf, cache_mem_k, cache_mem_v, mem_prompt, w_in, b_f, g_fox_out, g_gmlp_out, sgu_ln_g, sgu_ln_b, w_s, b_s, w_o, ln1_g, ln1_b, w_mq, w_mk, w_mv, w_mo, ln2_g, ln2_b, w_gate, w_up, w_down, ln3_g, ln3_b):
    depth = w_in.shape[0]
    B, S, D = x_prompt.shape
    Bd, T, _ = x_sample.shape
    P = cache_fox_k.shape[2]
    alpha = (2.0 * depth) ** 0.25
    assert D == D_MODEL and S % FOX_BLOCK == 0 and S % PROMPT_ROW_TILE == 0
    assert PROMPT_ROW_TILE % GMLP_CHUNK == 0 and (T & (T - 1)) == 0 and N_FOX_HEADS * T == LANES
    assert P % SAMPLE_KEY_CHUNK == 0 and S % TAIL_ROW_TILE == 0 and S % FFN_ROW_TILE == 0

    yp, ys = x_prompt, x_sample
    outs = [[] for _ in range(9)]
    for l in range(depth):
        w = w_in[l].astype(BF16)
        wq, wk, wv = (w[:, i * FOX_WIDTH:(i + 1) * FOX_WIDTH] for i in range(3))
        f0 = 3 * FOX_WIDTH
        wf = jnp.pad(w[:, f0:f0 + N_FOX_HEADS], ((0, 0), (0, LANES - N_FOX_HEADS)))
        bfp = jnp.pad(_row(b_f[l]), ((0, 0), (0, LANES - N_FOX_HEADS)))
        wg = w[:, f0 + N_FOX_HEADS:]
        lng, lnb, gout = _row(sgu_ln_g[l]), _row(sgu_ln_b[l]), _row(g_gmlp_out[l])
        tail_w = (_row(g_fox_out[l]), w_o[l].astype(BF16), _row(ln1_g[l]), _row(ln1_b[l]),
                  w_mq[l].astype(BF16), w_mo[l].astype(BF16), _row(ln2_g[l]), _row(ln2_b[l]))
        ffn_w = (w_gate[l].astype(BF16), w_up[l].astype(BF16), w_down[l].astype(BF16),
                 _row(ln3_g[l]), _row(ln3_b[l]))

        def mix_params(n):
            bias = jnp.repeat(b_s[l][:, :n].T, GMLP_GROUP_DIM, axis=1)
            return w_s[l][:, :n, :n], bias

        ws_p, bs_p = mix_params(GMLP_CHUNK)
        k, v, lf, hg, qt, kx, vxt, stats = _in_proj_mix(
            yp, wq, wk, wv, wf, bfp, wg, lng, lnb, ws_p, bs_p, gout,
            tm=PROMPT_ROW_TILE, chunk=GMLP_CHUNK, seg=None, prompt=True)
        att = _fox_prompt(qt, kx, vxt, stats, tb=FOX_BLOCK)
        mk, mv = _memory_kv(mem_prompt.reshape(B * N_MEM, D), w_mk[l].astype(BF16), w_mv[l].astype(BF16))
        mk, mv = mk.reshape(B, N_MEM, D), mv.reshape(B, N_MEM, D)
        x2 = _merge_xattn(yp, att, hg, mk, mv, *tail_w, tm=TAIL_ROW_TILE, alpha=alpha)
        yp = _ffn(x2, *ffn_w, tm=FFN_ROW_TILE, alpha=alpha)
        heads_last = lambda a: a.reshape(B, N_FOX_HEADS, FOX_HEAD_DIM, S).transpose(0, 3, 1, 2)
        outs[0].append(heads_last(k))
        outs[1].append(heads_last(v))
        outs[2].append(lf.transpose(0, 2, 1))
        outs[3].append(mk.reshape(B, N_MEM, N_MEM_HEADS, MEM_HEAD_DIM))
        outs[4].append(mv.reshape(B, N_MEM, N_MEM_HEADS, MEM_HEAD_DIM))

        ws_s, bs_s = mix_params(T)
        k, v, lf, hg, q, ct, gv = _in_proj_mix(
            ys.reshape(1, Bd * T, D), wq, wk, wv, wf, bfp, wg, lng, lnb, ws_s, bs_s, gout,
            tm=Bd * T, chunk=T, seg=T, prompt=False)
        per_b = lambda a: a.reshape(Bd, T, a.shape[-1])
        cn = ct.reshape(N_FOX_HEADS, Bd, T).transpose(1, 0, 2)
        cn = jnp.pad(cn, ((0, 0), (0, 0), (0, LANES - T)))
        dims_first = lambda a: a.transpose(0, 2, 3, 1).reshape(Bd, FOX_WIDTH, P)
        att = _fox_sample(per_b(q), per_b(k), per_b(v), cn,
                          dims_first(cache_fox_k[l]), dims_first(cache_fox_v[l]),
                          jnp.swapaxes(cache_fox_logf[l], 1, 2))
        x2 = _merge_xattn(ys, att, per_b(hg), cache_mem_k[l].reshape(Bd, N_MEM, D),
                          cache_mem_v[l].reshape(Bd, N_MEM, D), *tail_w, tm=T, alpha=alpha)
        ys = _ffn(x2.reshape(1, Bd * T, D), *ffn_w, tm=Bd * T, alpha=alpha).reshape(Bd, T, D)
        outs[5].append(k.reshape(Bd, T, N_FOX_HEADS, FOX_HEAD_DIM))
        outs[6].append(v.reshape(Bd, T, N_FOX_HEADS, FOX_HEAD_DIM))
        outs[7].append(per_b(lf))
        outs[8].append(per_b(gv))
    return (yp, ys) + tuple(jnp.stack(o) for o in outs)
```

```python
import functools

import jax
import jax.numpy as jnp
from jax import lax
from jax.experimental import pallas as pl
from jax.experimental.pallas import tpu as pltpu

F32 = jnp.float32
BF16 = jnp.bfloat16

LANES = 128
D_MODEL = 1024
STREAM_CHUNK = 64
N_FOX_HEADS = 8
FOX_HEAD_DIM = 64
FOX_WIDTH = N_FOX_HEADS * FOX_HEAD_DIM
HEADS_PER_STEP = LANES // FOX_HEAD_DIM
GMLP_WIDTH = D_MODEL - FOX_WIDTH
GMLP_GROUPS = 4
GMLP_GROUP_DIM = GMLP_WIDTH // GMLP_GROUPS
GMLP_CHUNK = 128
N_MEM = 256
N_MEM_HEADS = 4
MEM_HEAD_DIM = D_MODEL // N_MEM_HEADS
EPS = 1e-5
LOG2E = 1.4426950408889634
DECAY_TERMS = 3
DECAY_ROWS = 16
assert HEADS_PER_STEP * DECAY_TERMS <= DECAY_ROWS
N_TILE_STATS = 4
PRUNE_LOG2_GAP = 160.0
VMEM_LIMIT = 56 * 1024 * 1024

PROMPT_ROW_TILE = 512
FOX_BLOCK = 512
KV_UNROLL = 4
FOX_Q_BLOCKS = 4
TAIL_ROW_TILE = 1024
TAIL_PART_ROWS = 256
FFN_ROW_TILE = 1024
FFN_COL_CHUNK = 512
SAMPLE_KEY_CHUNK = 512


def _dot(a, b):
    return jnp.dot(a, b, preferred_element_type=F32)


def _dot_nt(a, b):
    return lax.dot_general(a, b, (((1,), (1,)), ((), ())), preferred_element_type=F32)


def _layer_norm(x, g, b):
    mu = jnp.mean(x, axis=-1, keepdims=True)
    xc = x - mu
    var = jnp.mean(xc * xc, axis=-1, keepdims=True)
    return xc * lax.rsqrt(var + EPS) * g + b


def _rms_norm(x, g):
    return x * lax.rsqrt(jnp.mean(x * x, axis=-1, keepdims=True) + EPS) * g


def _lane_cumsum(x, pos, length):
    shift = 1
    while shift < length:
        x = x + jnp.where(pos >= shift, pltpu.roll(x, shift, axis=1), 0.0)
        shift *= 2
    return x


def _div_pow2(x, n):
    assert n & (n - 1) == 0
    return x >> (n.bit_length() - 1)


def _row_parts(tm):
    n = tm // TAIL_PART_ROWS if tm % TAIL_PART_ROWS == 0 else 1
    return [slice(i * tm // n, (i + 1) * tm // n) for i in range(n)]


def _params(*semantics):
    return pltpu.CompilerParams(dimension_semantics=semantics, vmem_limit_bytes=VMEM_LIMIT)


def _in_proj_kernel(*refs, tm, chunk, seg, prompt):
    (x_ref, wq_ref, wk_ref, wv_ref, wf_ref, bf_ref, wg_ref, lng_ref, lnb_ref,
     ws_ref, bs_ref, gout_ref) = refs[:12]
    if prompt:
        k_ref, v_ref, lf_ref, hg_ref, qt_ref, kx_ref, vxt_ref, st_ref, carry_ref = refs[12:]
    else:
        k_ref, v_ref, lf_ref, hg_ref, q_ref, ct_ref, gv_ref = refs[12:]

    if prompt:
        @pl.when(pl.program_id(1) == 0)
        def _():
            carry_ref[...] = jnp.zeros_like(carry_ref)

    xb = x_ref[0].astype(BF16)
    zf = _dot(xb, wf_ref[...]) + bf_ref[...]
    zg = _dot(xb, wg_ref[...])
    zq = _dot(xb, wq_ref[...])
    if prompt:
        qtb = (zq * (FOX_HEAD_DIM ** -0.5 * LOG2E)).T.astype(BF16)
        qt_ref[0] = qtb
    else:
        q_ref[0] = (zq * (FOX_HEAD_DIM ** -0.5)).astype(BF16)
    zk = _dot(xb, wk_ref[...])
    zv = _dot(xb, wv_ref[...])
    lf = jax.nn.log_sigmoid(zf)
    lft = lf.T[:N_FOX_HEADS, :]
    if prompt:
        zvt = zv.T
        zkt = zk.T
        k_ref[0] = zkt
        v_ref[0] = zvt
        lf_ref[0] = lft
    else:
        k_ref[0] = zk
        v_ref[0] = zv
        lf_ref[0] = lf[:, :N_FOX_HEADS]
    lane = lax.broadcasted_iota(jnp.int32, (N_FOX_HEADS, tm), 1)
    if prompt:
        c = _lane_cumsum(lft, lane, tm) + carry_ref[...]
        carry_ref[...] = jnp.broadcast_to(c[:, tm - 1:tm], (N_FOX_HEADS, tm))

        decay = c * (-LOG2E)

        def max_head_norms(xt):
            sq = xt * xt
            hid = lax.broadcasted_iota(jnp.int32, (N_FOX_HEADS, LANES), 0)
            out = jnp.zeros((N_FOX_HEADS, LANES), F32)
            for hd in range(N_FOX_HEADS):
                n2 = jnp.sum(sq[hd * FOX_HEAD_DIM:(hd + 1) * FOX_HEAD_DIM, :], axis=0, keepdims=True)
                out = jnp.where(hid == hd, jnp.max(jnp.sqrt(n2), axis=1, keepdims=True), out)
            return out

        st_ref[0, 0, 0 * N_FOX_HEADS:1 * N_FOX_HEADS, :] = max_head_norms(qtb.astype(F32))
        st_ref[0, 0, 1 * N_FOX_HEADS:2 * N_FOX_HEADS, :] = max_head_norms(zkt.astype(BF16).astype(F32))
        st_ref[0, 0, 2 * N_FOX_HEADS:3 * N_FOX_HEADS, :] = jnp.broadcast_to(decay[:, 0:1], (N_FOX_HEADS, LANES))
        st_ref[0, 0, 3 * N_FOX_HEADS:4 * N_FOX_HEADS, :] = jnp.broadcast_to(decay[:, tm - 1:tm], (N_FOX_HEADS, LANES))

        rest = decay
        terms = []
        for _ in range(DECAY_TERMS):
            t = rest.astype(BF16).astype(F32)
            terms.append(t)
            rest = rest - t
        rid = lax.broadcasted_iota(jnp.int32, (DECAY_ROWS, tm), 0)
        for g in range(FOX_WIDTH // LANES):
            dec = jnp.zeros((DECAY_ROWS, tm), F32)
            for hh in range(HEADS_PER_STEP):
                hd = g * HEADS_PER_STEP + hh
                for ti, t in enumerate(terms):
                    dec = jnp.where(rid == hh * DECAY_TERMS + ti, t[hd:hd + 1, :], dec)
            dec = jnp.concatenate([dec, jnp.zeros((LANES - DECAY_ROWS, tm), F32)], axis=0)
            kx_ref[0, :, 2 * g * LANES:(2 * g + 1) * LANES] = zk[:, g * LANES:(g + 1) * LANES].astype(BF16)
            kx_ref[0, :, (2 * g + 1) * LANES:(2 * g + 2) * LANES] = dec.T.astype(BF16)
        for hd in range(N_FOX_HEADS):
            vxt_ref[0, hd * LANES:hd * LANES + FOX_HEAD_DIM, :] = (
                zvt[hd * FOX_HEAD_DIM:(hd + 1) * FOX_HEAD_DIM, :].astype(BF16))
            vxt_ref[0, hd * LANES + FOX_HEAD_DIM:(hd + 1) * LANES, :] = jnp.ones((LANES - FOX_HEAD_DIM, tm), BF16)
    else:
        ct_ref[0] = _lane_cumsum(lft, lane & (seg - 1), seg)

    z = jax.nn.gelu(zg)
    u = z[:, :GMLP_WIDTH]
    vn = _layer_norm(z[:, GMLP_WIDTH:], lng_ref[...], lnb_ref[...])
    if not prompt:
        gv_ref[0] = vn
    vnb = vn.astype(BF16)
    ri = lax.broadcasted_iota(jnp.int32, (chunk, chunk), 0)
    ci = lax.broadcasted_iota(jnp.int32, (chunk, chunk), 1)
    causal = _div_pow2(ci, STREAM_CHUNK) <= _div_pow2(ri, STREAM_CHUNK)
    wm = [jnp.where(causal, ws_ref[g], 0.0).astype(BF16) for g in range(GMLP_GROUPS)]
    for r0 in range(0, tm, chunk):
        mixed = jnp.concatenate(
            [_dot(wm[g], vnb[r0:r0 + chunk, g * GMLP_GROUP_DIM:(g + 1) * GMLP_GROUP_DIM])
             for g in range(GMLP_GROUPS)], axis=1) + bs_ref[...]
        gm = u[r0:r0 + chunk, :] * mixed
        hg_ref[0, r0:r0 + chunk, :] = _rms_norm(gm, gout_ref[...]).astype(BF16)


def _in_proj_mix(x, wq, wk, wv, wf, bfp, wg, lng, lnb, ws, bs_full, gout, *, tm, chunk, seg, prompt):
    G, R, D = x.shape
    grid = (G, R // tm)
    row = lambda w: pl.BlockSpec((1, tm, w), lambda g, j: (g, j, 0))
    full = lambda a: pl.BlockSpec(a.shape, lambda g, j: (0,) * a.ndim)
    in_specs = [row(D)] + [full(a) for a in (wq, wk, wv, wf, bfp, wg, lng, lnb, ws, bs_full, gout)]
    col = lambda h: pl.BlockSpec((1, h, tm), lambda g, j: (g, 0, j))
    wide = (lambda w: (G, w, R)) if prompt else (lambda w: (G, R, w))
    spec = col if prompt else row
    out_shape = [
        jax.ShapeDtypeStruct(wide(FOX_WIDTH), F32),
        jax.ShapeDtypeStruct(wide(FOX_WIDTH), F32),
        jax.ShapeDtypeStruct(wide(N_FOX_HEADS), F32),
        jax.ShapeDtypeStruct((G, R, GMLP_WIDTH), BF16),
    ]
    out_specs = [spec(FOX_WIDTH), spec(FOX_WIDTH), spec(N_FOX_HEADS), row(GMLP_WIDTH)]
    scratch = []
    if prompt:
        out_shape += [
            jax.ShapeDtypeStruct((G, FOX_WIDTH, R), BF16),
            jax.ShapeDtypeStruct((G, R, 2 * FOX_WIDTH), BF16),
            jax.ShapeDtypeStruct((G, N_FOX_HEADS * LANES, R), BF16),
            jax.ShapeDtypeStruct((G, R // tm, N_TILE_STATS * N_FOX_HEADS, LANES), F32)]
        out_specs += [col(FOX_WIDTH), row(2 * FOX_WIDTH), col(N_FOX_HEADS * LANES),
                      pl.BlockSpec((1, 1, N_TILE_STATS * N_FOX_HEADS, LANES), lambda g, j: (g, j, 0, 0))]
        scratch = [pltpu.VMEM((N_FOX_HEADS, tm), F32)]
    else:
        out_shape += [jax.ShapeDtypeStruct((G, R, FOX_WIDTH), BF16),
                      jax.ShapeDtypeStruct((G, N_FOX_HEADS, R), F32),
                      jax.ShapeDtypeStruct((G, R, GMLP_WIDTH), F32)]
        out_specs += [row(FOX_WIDTH), col(N_FOX_HEADS), row(GMLP_WIDTH)]
    return pl.pallas_call(
        functools.partial(_in_proj_kernel, tm=tm, chunk=chunk, seg=seg, prompt=prompt),
        grid=grid, in_specs=in_specs, out_specs=out_specs, out_shape=out_shape,
        scratch_shapes=scratch, compiler_params=_params("arbitrary", "arbitrary"),
        name="in_proj_mix_prompt" if prompt else "in_proj_mix_sample",
    )(x, wq, wk, wv, wf, bfp, wg, lng, lnb, ws, bs_full, gout)


def _fox_prompt_kernel(first_ref, qt_ref, kx_ref, vxt_ref, o_ref, qx_ref, m_ref, acc_ref, s_ref, bm_ref, *, tb):
    step = pl.program_id(2)
    flat = (pl.program_id(0) * pl.num_programs(1) + pl.program_id(1)) * pl.num_programs(2) + step
    dim = lax.broadcasted_iota(jnp.int32, (LANES, tb), 0)
    key = lax.broadcasted_iota(jnp.int32, (tb, tb), 0)
    qry = lax.broadcasted_iota(jnp.int32, (tb, tb), 1)
    causal = key <= qry
    for a in range(FOX_Q_BLOCKS):
        qt = qt_ref[0, :, a * tb:(a + 1) * tb]
        for h in range(HEADS_PER_STEP):
            own = _div_pow2(dim, FOX_HEAD_DIM) == h
            decay = (dim >= h * DECAY_TERMS) & (dim < (h + 1) * DECAY_TERMS)
            qx_ref[a, h, :LANES, :] = jnp.where(own, qt, jnp.zeros_like(qt))
            qx_ref[a, h, LANES:, :] = jnp.where(decay, 1.0, 0.0).astype(BF16)

    def reset():
        m_ref[...] = jnp.full_like(m_ref, -jnp.inf)
        acc_ref[...] = jnp.zeros_like(acc_ref)

    def scores(a, j):
        kx = kx_ref[0, pl.ds(pl.multiple_of(j * tb, tb), tb), :]
        return [_dot(kx, qx_ref[a, h]) for h in range(HEADS_PER_STEP)]

    def diagonal_scores(a, j):
        half = tb // 2
        kx = kx_ref[0, pl.ds(pl.multiple_of(j * tb, tb), tb), :]
        out = []
        for h in range(HEADS_PER_STEP):
            left = _dot(kx[:half], qx_ref[a, h, :, :half])
            left = jnp.concatenate([left, jnp.full((half, half), -jnp.inf, F32)], axis=0)
            right = _dot(kx, qx_ref[a, h, :, half:])
            out.append(jnp.where(causal, jnp.concatenate([left, right], axis=1), -jnp.inf))
        return out

    def stash(s_new):
        for h in range(HEADS_PER_STEP):
            s_ref[h] = s_new[h]
            bm_ref[h] = jnp.max(s_new[h], axis=0, keepdims=True)

    def consume(j):
        start = pl.multiple_of(j * tb, tb)
        for h in range(HEADS_PER_STEP):
            m_prev = m_ref[h]
            m_new = jnp.maximum(m_prev, bm_ref[h])
            p = jnp.exp2(s_ref[h] - m_new)
            pv = _dot(vxt_ref[0, h * LANES:(h + 1) * LANES, pl.ds(start, tb)], p.astype(BF16))
            acc_ref[h] = jnp.exp2(m_prev - m_new) * acc_ref[h] + pv
            m_ref[h] = m_new

    def finalize(a):
        outs = []
        for h in range(HEADS_PER_STEP):
            acc = acc_ref[h]
            outs.append(acc[:FOX_HEAD_DIM, :] / acc[FOX_HEAD_DIM:FOX_HEAD_DIM + 1, :])
        o_ref[0, a * tb:(a + 1) * tb, :] = jnp.concatenate(outs, axis=0).T

    reset()
    stash(diagonal_scores(0, step * FOX_Q_BLOCKS))
    for a in range(FOX_Q_BLOCKS):
        qi = step * FOX_Q_BLOCKS + a
        first = first_ref[flat * FOX_Q_BLOCKS + a]

        def pipelined(t, a=a, qi=qi, first=first):
            s_next = scores(a, t)
            consume(jnp.where(t == first, qi, t - 1))
            stash(s_next)

        def unrolled(t, carry, pipelined=pipelined, first=first):
            for u in range(KV_UNROLL):
                pipelined(first + KV_UNROLL * t + u)
            return carry

        def single(t, carry, pipelined=pipelined):
            pipelined(t)
            return carry

        n_unrolled = _div_pow2(qi - first, KV_UNROLL)
        lax.fori_loop(0, n_unrolled, unrolled, 0)
        lax.fori_loop(first + n_unrolled * KV_UNROLL, qi, single, 0)
        last = jnp.where(qi > first, qi - 1, qi)
        if a + 1 < FOX_Q_BLOCKS:
            s_next = diagonal_scores(a + 1, qi + 1)
            consume(last)
            finalize(a)
            reset()
            stash(s_next)
        else:
            consume(last)
            finalize(a)


def _first_live_block(stats):
    st = stats[..., 0]
    B, nblk, _ = st.shape
    qn, kn, d_first, d_last = (st[:, :, i * N_FOX_HEADS:(i + 1) * N_FOX_HEADS] for i in range(N_TILE_STATS))
    bound = (qn[:, :, None] * (kn[:, None, :] + kn[:, :, None])
             + d_last[:, None, :] - d_first[:, :, None])
    dead = bound < -PRUNE_LOG2_GAP
    dead = dead.reshape(B, nblk, nblk, N_FOX_HEADS // HEADS_PER_STEP, HEADS_PER_STEP).all(axis=-1)
    below = jnp.arange(nblk)[None, :] < jnp.arange(nblk)[:, None]
    dead = jnp.logical_and(dead, below[None, :, :, None])
    first = jnp.min(jnp.where(dead, nblk, jnp.arange(nblk)[None, None, :, None]), axis=2)
    return first.astype(jnp.int32).transpose(0, 2, 1).reshape(-1)


def _fox_prompt(qt, kx, vxt, stats, *, tb):
    B, _, S = qt.shape
    tq = FOX_Q_BLOCKS * tb
    assert stats.shape[1] == S // tb and S % tq == 0
    return pl.pallas_call(
        functools.partial(_fox_prompt_kernel, tb=tb),
        grid_spec=pltpu.PrefetchScalarGridSpec(
            num_scalar_prefetch=1,
            grid=(B, FOX_WIDTH // LANES, S // tq),
            in_specs=[
                pl.BlockSpec((1, LANES, tq), lambda b, g, i, first: (b, g, i)),
                pl.BlockSpec((1, S, 2 * LANES), lambda b, g, i, first: (b, 0, g)),
                pl.BlockSpec((1, 2 * LANES, S), lambda b, g, i, first: (b, g, 0)),
            ],
            out_specs=pl.BlockSpec((1, tq, LANES), lambda b, g, i, first: (b, i, g)),
            scratch_shapes=[pltpu.VMEM((FOX_Q_BLOCKS, HEADS_PER_STEP, 2 * LANES, tb), BF16),
                            pltpu.VMEM((HEADS_PER_STEP, 1, tb), F32),
                            pltpu.VMEM((HEADS_PER_STEP, LANES, tb), F32),
                            pltpu.VMEM((HEADS_PER_STEP, tb, tb), F32),
                            pltpu.VMEM((HEADS_PER_STEP, 1, tb), F32)]),
        out_shape=jax.ShapeDtypeStruct((B, S, FOX_WIDTH), F32),
        compiler_params=_params("arbitrary", "arbitrary", "arbitrary"),
        name="fox_prompt",
    )(_first_live_block(stats), qt, kx, vxt)


def _fox_sample_kernel(q_ref, kn_ref, vn_ref, cn_ref, ck_ref, cv_ref, lft_ref, o_ref, s_ref, *, P, T):
    rows = N_FOX_HEADS * T
    q = q_ref[0]
    qt = jnp.concatenate([q] * N_FOX_HEADS, axis=0)
    r = lax.broadcasted_iota(jnp.int32, (rows, FOX_WIDTH), 0)
    c = lax.broadcasted_iota(jnp.int32, (rows, FOX_WIDTH), 1)
    wt = jnp.where(_div_pow2(r, T) == _div_pow2(c, FOX_HEAD_DIM), qt, jnp.zeros_like(qt))

    lane = lax.broadcasted_iota(jnp.int32, (N_FOX_HEADS, P), 1)
    cs = _lane_cumsum(lft_ref[0], lane, P)
    suffix = cs[:, P - 1:P] - cs

    def head_rows(x):
        return jnp.concatenate(
            [jnp.broadcast_to(x[h:h + 1, :], (T, x.shape[1])) for h in range(N_FOX_HEADS)], axis=0)

    kc_w = SAMPLE_KEY_CHUNK
    for k0 in range(0, P, kc_w):
        kct = ck_ref[0, :, k0:k0 + kc_w].astype(BF16)
        s_ref[:, k0:k0 + kc_w] = _dot(wt, kct) + head_rows(suffix[:, k0:k0 + kc_w])
    pad = jnp.zeros((LANES - T, FOX_WIDTH), BF16)
    sn = _dot_nt(wt, jnp.concatenate([kn_ref[0].astype(BF16), pad], axis=0))
    rr = lax.broadcasted_iota(jnp.int32, (rows, LANES), 0)
    cc = lax.broadcasted_iota(jnp.int32, (rows, LANES), 1)
    s_ref[:, P:P + LANES] = jnp.where(cc <= (rr & (T - 1)), sn - head_rows(cn_ref[0]), -jnp.inf)

    n_cols = P + LANES
    m = jnp.full((rows, 1), -jnp.inf, F32)
    for k0 in range(0, n_cols, kc_w):
        w = min(kc_w, n_cols - k0)
        m = jnp.maximum(m, jnp.max(s_ref[:, k0:k0 + w], axis=1, keepdims=True))
    l = jnp.zeros((rows, 1), F32)
    acc = jnp.zeros((rows, FOX_WIDTH), F32)
    for k0 in range(0, P, kc_w):
        p = jnp.exp(s_ref[:, k0:k0 + kc_w] - m)
        l = l + jnp.sum(p, axis=1, keepdims=True)
        acc = acc + _dot_nt(p.astype(BF16), cv_ref[0, :, k0:k0 + kc_w].astype(BF16))
    p = jnp.exp(s_ref[:, P:P + LANES] - m)
    l = l + jnp.sum(p, axis=1, keepdims=True)
    acc = acc + _dot(p.astype(BF16), jnp.concatenate([vn_ref[0].astype(BF16), pad], axis=0))
    on = acc / l
    c16 = lax.broadcasted_iota(jnp.int32, (T, FOX_WIDTH), 1)
    out = jnp.zeros((T, FOX_WIDTH), F32)
    for h in range(N_FOX_HEADS):
        out = out + jnp.where(_div_pow2(c16, FOX_HEAD_DIM) == h, on[h * T:(h + 1) * T, :], 0.0)
    o_ref[0] = out


def _fox_sample(q, kn, vn, cn_pad, cache_k, cache_v, cache_lft):
    Bd, T, _ = q.shape
    P = cache_k.shape[2]
    per_b = lambda shape: pl.BlockSpec((1,) + shape, lambda b: (b, 0, 0))
    return pl.pallas_call(
        functools.partial(_fox_sample_kernel, P=P, T=T),
        grid=(Bd,),
        in_specs=[per_b((T, FOX_WIDTH)), per_b((T, FOX_WIDTH)), per_b((T, FOX_WIDTH)),
                  per_b((N_FOX_HEADS, LANES)), per_b((FOX_WIDTH, P)), per_b((FOX_WIDTH, P)),
                  per_b((N_FOX_HEADS, P))],
        out_specs=per_b((T, FOX_WIDTH)),
        out_shape=jax.ShapeDtypeStruct((Bd, T, FOX_WIDTH), F32),
        scratch_shapes=[pltpu.VMEM((N_FOX_HEADS * T, P + LANES), F32)],
        compiler_params=_params("arbitrary"),
        name="fox_sample",
    )(q, kn, vn, cn_pad, cache_k, cache_v, cache_lft)


def _memory_kv_kernel(mem_ref, wk_ref, wv_ref, k_ref, v_ref):
    mb = mem_ref[...].astype(BF16)
    k_ref[...] = _dot(mb, wk_ref[...])
    v_ref[...] = _dot(mb, wv_ref[...])


def _memory_kv(mem2d, wk, wv):
    n = mem2d.shape[0]
    full = lambda a: pl.BlockSpec(a.shape, lambda i: (0, 0))
    out = jax.ShapeDtypeStruct((n, D_MODEL), F32)
    return pl.pallas_call(
        _memory_kv_kernel, grid=(1,),
        in_specs=[full(mem2d), full(wk), full(wv)],
        out_specs=[pl.BlockSpec((n, D_MODEL), lambda i: (0, 0))] * 2,
        out_shape=[out, out], compiler_params=_params("arbitrary"), name="memory_kv",
    )(mem2d, wk, wv)


def _merge_xattn_kernel(x_ref, att_ref, hg_ref, mk_ref, mv_ref, gfox_ref, wo_ref, ln1g_ref, ln1b_ref,
                        wmq_ref, wmo_ref, ln2g_ref, ln2b_ref, o_ref, *, alpha):
    parts = _row_parts(x_ref.shape[1])
    head = lambda hh: slice(hh * MEM_HEAD_DIM, (hh + 1) * MEM_HEAD_DIM)
    h = [jnp.concatenate([_rms_norm(att_ref[0, r, :], gfox_ref[...]).astype(BF16), hg_ref[0, r, :]], axis=1)
         for r in parts]
    mix = [_dot(hi, wo_ref[...]) for hi in h]
    x1 = [_layer_norm(alpha * x_ref[0, r, :] + mi, ln1g_ref[...], ln1b_ref[...]) for r, mi in zip(parts, mix)]
    qm = [_dot(xi.astype(BF16), wmq_ref[...]) for xi in x1]
    qm = [(qi * (MEM_HEAD_DIM ** -0.5)).astype(BF16) for qi in qm]
    if len(mk_ref.shape) == 4:
        n_rows = N_MEM * N_MEM_HEADS
        mk = mk_ref[0].reshape(n_rows, MEM_HEAD_DIM).astype(BF16)
        mv = mv_ref[0].reshape(n_rows, MEM_HEAD_DIM).astype(BF16)
        own = lambda hh, sh: (lax.broadcasted_iota(jnp.int32, sh.shape, 1) & (N_MEM_HEADS - 1)) == hh
        s = [[_dot_nt(qi[:, head(hh)], mk) for hh in range(N_MEM_HEADS)] for qi in qm]
        s = [[jnp.where(own(hh, sh), sh, -jnp.inf) for hh, sh in enumerate(si)] for si in s]
        values = lambda hh: mv
    else:
        mk = mk_ref[0].astype(BF16)
        mv = mv_ref[0].astype(BF16)
        s = [[_dot_nt(qi[:, head(hh)], mk[:, head(hh)]) for hh in range(N_MEM_HEADS)] for qi in qm]
        values = lambda hh: mv[:, head(hh)]
    p = [[jnp.exp(sh - jnp.max(sh, axis=1, keepdims=True)) for sh in si] for si in s]
    o = [[_dot(ph.astype(BF16), values(hh)) / jnp.sum(ph, axis=1, keepdims=True)
          for hh, ph in enumerate(pi)] for pi in p]
    y = [_dot(jnp.concatenate(oi, axis=1).astype(BF16), wmo_ref[...]) for oi in o]
    for r, xi, yi in zip(parts, x1, y):
        o_ref[0, r, :] = _layer_norm(alpha * xi + yi, ln2g_ref[...], ln2b_ref[...])


def _merge_xattn(x, att, hg, mk, mv, gfox, wo, ln1g, ln1b, wmq, wmo, ln2g, ln2b, *, tm, alpha):
    G, R, D = x.shape
    row = lambda w: pl.BlockSpec((1, tm, w), lambda g, j: (g, j, 0))
    mem = pl.BlockSpec((1,) + mk.shape[1:], lambda g, j: (g,) + (0,) * (mk.ndim - 1))
    full = lambda a: pl.BlockSpec(a.shape, lambda g, j: (0,) * a.ndim)
    return pl.pallas_call(
        functools.partial(_merge_xattn_kernel, alpha=alpha),
        grid=(G, R // tm),
        in_specs=[row(D), row(FOX_WIDTH), row(GMLP_WIDTH), mem, mem] +
                 [full(a) for a in (gfox, wo, ln1g, ln1b, wmq, wmo, ln2g, ln2b)],
        out_specs=row(D), out_shape=jax.ShapeDtypeStruct((G, R, D), F32),
        compiler_params=_params("arbitrary", "arbitrary"), name=f"merge_xattn_{tm}",
    )(x, att, hg, mk, mv, gfox, wo, ln1g, ln1b, wmq, wmo, ln2g, ln2b)


def _ffn_kernel(x_ref, wg_ref, wu_ref, wd_ref, g_ref, b_ref, o_ref, h_ref, *, alpha, d_ff):
    parts = _row_parts(x_ref.shape[1])
    xb = [x_ref[0, r, :].astype(BF16) for r in parts]
    for c0 in range(0, d_ff, FFN_COL_CHUNK):
        cols = slice(c0, min(c0 + FFN_COL_CHUNK, d_ff))
        for r, xi in zip(parts, xb):
            gate = _dot(xi, wg_ref[:, cols])
            up = _dot(xi, wu_ref[:, cols])
            h_ref[r, cols] = (jax.nn.silu(gate) * up).astype(BF16)
    y = [_dot(h_ref[r, :], wd_ref[...]) for r in parts]
    for r, yi in zip(parts, y):
        o_ref[0, r, :] = _layer_norm(alpha * x_ref[0, r, :] + yi, g_ref[...], b_ref[...])


def _ffn(x, wg, wu, wd, g, b, *, tm, alpha):
    G, R, D = x.shape
    d_ff = wg.shape[1]
    assert d_ff % LANES == 0
    row = pl.BlockSpec((1, tm, D), lambda gi, j: (gi, j, 0))
    full = lambda a: pl.BlockSpec(a.shape, lambda gi, j: (0,) * a.ndim, pipeline_mode=pl.Buffered(1))
    return pl.pallas_call(
        functools.partial(_ffn_kernel, alpha=alpha, d_ff=d_ff),
        grid=(G, R // tm),
        in_specs=[row] + [full(a) for a in (wg, wu, wd, g, b)],
        out_specs=row, out_shape=jax.ShapeDtypeStruct((G, R, D), F32),
        scratch_shapes=[pltpu.VMEM((tm, d_ff), BF16)],
        compiler_params=_params("arbitrary", "arbitrary"), name=f"ffn_{tm}",
    )(x, wg, wu, wd, g, b)


def _row(a):
    return a.reshape(1, -1)


def kernel(x_prompt, x_sample, cache_fox_k, cache_fox_v, cache_fox_logf, cache_mem_k, cache_mem_v, mem_prompt, w_in, b_f, g_fox_out, g_gmlp_out, sgu_ln_g, sgu_ln_b, w_s, b_s, w_o, ln1_g, ln1_b, w_mq, w_mk, w_mv, w_mo, ln2_g, ln2_b, w_gate, w_up, w_down, ln3_g, ln3_b):
    depth = w_in.shape[0]
    B, S, D = x_prompt.shape
    Bd, T, _ = x_sample.shape
    P = cache_fox_k.shape[2]
    alpha = (2.0 * depth) ** 0.25
    assert D == D_MODEL and S % FOX_BLOCK == 0 and S % PROMPT_ROW_TILE == 0
    assert PROMPT_ROW_TILE % GMLP_CHUNK == 0 and (T & (T - 1)) == 0 and N_FOX_HEADS * T == LANES
    assert P % SAMPLE_KEY_CHUNK == 0 and S % TAIL_ROW_TILE == 0 and S % FFN_ROW_TILE == 0

    yp, ys = x_prompt, x_sample
    outs = [[] for _ in range(9)]
    for l in range(depth):
        w = w_in[l].astype(BF16)
        wq, wk, wv = (w[:, i * FOX_WIDTH:(i + 1) * FOX_WIDTH] for i in range(3))
        f0 = 3 * FOX_WIDTH
        wf = jnp.pad(w[:, f0:f0 + N_FOX_HEADS], ((0, 0), (0, LANES - N_FOX_HEADS)))
        bfp = jnp.pad(_row(b_f[l]), ((0, 0), (0, LANES - N_FOX_HEADS)))
        wg = w[:, f0 + N_FOX_HEADS:]
        lng, lnb, gout = _row(sgu_ln_g[l]), _row(sgu_ln_b[l]), _row(g_gmlp_out[l])
        tail_w = (_row(g_fox_out[l]), w_o[l].astype(BF16), _row(ln1_g[l]), _row(ln1_b[l]),
                  w_mq[l].astype(BF16), w_mo[l].astype(BF16), _row(ln2_g[l]), _row(ln2_b[l]))
        ffn_w = (w_gate[l].astype(BF16), w_up[l].astype(BF16), w_down[l].astype(BF16),
                 _row(ln3_g[l]), _row(ln3_b[l]))

        def mix_params(n):
            bias = jnp.repeat(b_s[l][:, :n].T, GMLP_GROUP_DIM, axis=1)
            return w_s[l][:, :n, :n], bias

        ws_p, bs_p = mix_params(GMLP_CHUNK)
        k, v, lf, hg, qt, kx, vxt, stats = _in_proj_mix(
            yp, wq, wk, wv, wf, bfp, wg, lng, lnb, ws_p, bs_p, gout,
            tm=PROMPT_ROW_TILE, chunk=GMLP_CHUNK, seg=None, prompt=True)
        att = _fox_prompt(qt, kx, vxt, stats, tb=FOX_BLOCK)
        mk, mv = _memory_kv(mem_prompt.reshape(B * N_MEM, D), w_mk[l].astype(BF16), w_mv[l].astype(BF16))
        mk, mv = mk.reshape(B, N_MEM, D), mv.reshape(B, N_MEM, D)
        x2 = _merge_xattn(yp, att, hg, mk, mv, *tail_w, tm=TAIL_ROW_TILE, alpha=alpha)
        yp = _ffn(x2, *ffn_w, tm=FFN_ROW_TILE, alpha=alpha)
        heads_last = lambda a: a.reshape(B, N_FOX_HEADS, FOX_HEAD_DIM, S).transpose(0, 3, 1, 2)
        outs[0].append(heads_last(k))
        outs[1].append(heads_last(v))
        outs[2].append(lf.transpose(0, 2, 1))
        outs[3].append(mk.reshape(B, N_MEM, N_MEM_HEADS, MEM_HEAD_DIM))
        outs[4].append(mv.reshape(B, N_MEM, N_MEM_HEADS, MEM_HEAD_DIM))

        ws_s, bs_s = mix_params(T)
        k, v, lf, hg, q, ct, gv = _in_proj_mix(
            ys.reshape(1, Bd * T, D), wq, wk, wv, wf, bfp, wg, lng, lnb, ws_s, bs_s, gout,
            tm=Bd * T, chunk=T, seg=T, prompt=False)
        per_b = lambda a: a.reshape(Bd, T, a.shape[-1])
        cn = ct.reshape(N_FOX_HEADS, Bd, T).transpose(1, 0, 2)
        cn = jnp.pad(cn, ((0, 0), (0, 0), (0, LANES - T)))
        dims_first = lambda a: a.transpose(0, 2, 3, 1).reshape(Bd, FOX_WIDTH, P)
        att = _fox_sample(per_b(q), per_b(k), per_b(v), cn,
                          dims_first(cache_fox_k[l]), dims_first(cache_fox_v[l]),
                          jnp.swapaxes(cache_fox_logf[l], 1, 2))
        x2 = _merge_xattn(ys, att, per_b(hg), cache_mem_k[l], cache_mem_v[l], *tail_w, tm=T, alpha=alpha)
        ys = _ffn(x2.reshape(1, Bd * T, D), *ffn_w, tm=Bd * T, alpha=alpha).reshape(Bd, T, D)
        outs[5].append(k.reshape(Bd, T, N_FOX_HEADS, FOX_HEAD_DIM))
        outs[6].append(v.reshape(Bd, T, N_FOX_HEADS, FOX_HEAD_DIM))
        outs[7].append(per_b(lf))
        outs[8].append(per_b(gv))
    return (yp, ys) + tuple(jnp.stack(o) for o in outs)
```

```python
import functools

import jax
import jax.numpy as jnp
from jax import lax
from jax.experimental import pallas as pl
from jax.experimental.pallas import tpu as pltpu

F32 = jnp.float32
BF16 = jnp.bfloat16

LANES = 128
D_MODEL = 1024
STREAM_CHUNK = 64
N_FOX_HEADS = 8
FOX_HEAD_DIM = 64
FOX_WIDTH = N_FOX_HEADS * FOX_HEAD_DIM
HEADS_PER_STEP = LANES // FOX_HEAD_DIM
GMLP_WIDTH = D_MODEL - FOX_WIDTH
GMLP_GROUPS = 4
GMLP_GROUP_DIM = GMLP_WIDTH // GMLP_GROUPS
GMLP_CHUNK = 128
N_MEM = 256
N_MEM_HEADS = 4
MEM_HEAD_DIM = D_MODEL // N_MEM_HEADS
EPS = 1e-5
LOG2E = 1.4426950408889634
DECAY_TERMS = 3
DECAY_ROWS = 16
assert HEADS_PER_STEP * DECAY_TERMS <= DECAY_ROWS
N_TILE_STATS = 4
PRUNE_LOG2_GAP = 160.0
VMEM_LIMIT = 56 * 1024 * 1024

PROMPT_ROW_TILE = 512
FOX_BLOCK = 512
KV_UNROLL = 4
FOX_Q_BLOCKS = 4
TAIL_ROW_TILE = 1024
TAIL_PART_ROWS = 256
FFN_ROW_TILE = 1024
FFN_COL_CHUNK = 512
SAMPLE_KEY_CHUNK = 512


def _dot(a, b):
    return jnp.dot(a, b, preferred_element_type=F32)


def _dot_nt(a, b):
    return lax.dot_general(a, b, (((1,), (1,)), ((), ())), preferred_element_type=F32)


def _layer_norm(x, g, b):
    mu = jnp.mean(x, axis=-1, keepdims=True)
    xc = x - mu
    var = jnp.mean(xc * xc, axis=-1, keepdims=True)
    return xc * lax.rsqrt(var + EPS) * g + b


def _rms_norm(x, g):
    return x * lax.rsqrt(jnp.mean(x * x, axis=-1, keepdims=True) + EPS) * g


def _lane_cumsum(x, pos, length):
    shift = 1
    while shift < length:
        x = x + jnp.where(pos >= shift, pltpu.roll(x, shift, axis=1), 0.0)
        shift *= 2
    return x


def _div_pow2(x, n):
    assert n & (n - 1) == 0
    return x >> (n.bit_length() - 1)


def _row_parts(tm):
    n = tm // TAIL_PART_ROWS if tm % TAIL_PART_ROWS == 0 else 1
    return [slice(i * tm // n, (i + 1) * tm // n) for i in range(n)]


def _params(*semantics):
    return pltpu.CompilerParams(dimension_semantics=semantics, vmem_limit_bytes=VMEM_LIMIT)


def _in_proj_kernel(*refs, tm, chunk, seg, prompt):
    (x_ref, wq_ref, wk_ref, wv_ref, wf_ref, bf_ref, wg_ref, lng_ref, lnb_ref,
     ws_ref, bs_ref, gout_ref) = refs[:12]
    if prompt:
        k_ref, v_ref, lf_ref, hg_ref, qt_ref, kx_ref, vxt_ref, st_ref, carry_ref = refs[12:]
    else:
        k_ref, v_ref, lf_ref, hg_ref, q_ref, ct_ref, gv_ref = refs[12:]

    if prompt:
        @pl.when(pl.program_id(1) == 0)
        def _():
            carry_ref[...] = jnp.zeros_like(carry_ref)

    xb = x_ref[0].astype(BF16)
    zf = _dot(xb, wf_ref[...]) + bf_ref[...]
    zg = _dot(xb, wg_ref[...])
    zq = _dot(xb, wq_ref[...])
    if prompt:
        qtb = (zq * (FOX_HEAD_DIM ** -0.5 * LOG2E)).T.astype(BF16)
        qt_ref[0] = qtb
    else:
        q_ref[0] = (zq * (FOX_HEAD_DIM ** -0.5)).astype(BF16)
    zk = _dot(xb, wk_ref[...])
    zv = _dot(xb, wv_ref[...])
    lf = jax.nn.log_sigmoid(zf)
    lft = lf.T[:N_FOX_HEADS, :]
    if prompt:
        zvt = zv.T
        zkt = zk.T
        k_ref[0] = zkt
        v_ref[0] = zvt
        lf_ref[0] = lft
    else:
        k_ref[0] = zk
        v_ref[0] = zv
        lf_ref[0] = lf[:, :N_FOX_HEADS]
    lane = lax.broadcasted_iota(jnp.int32, (N_FOX_HEADS, tm), 1)
    if prompt:
        c = _lane_cumsum(lft, lane, tm) + carry_ref[...]
        carry_ref[...] = jnp.broadcast_to(c[:, tm - 1:tm], (N_FOX_HEADS, tm))

        decay = c * (-LOG2E)

        def max_head_norms(xt):
            sq = xt * xt
            hid = lax.broadcasted_iota(jnp.int32, (N_FOX_HEADS, LANES), 0)
            out = jnp.zeros((N_FOX_HEADS, LANES), F32)
            for hd in range(N_FOX_HEADS):
                n2 = jnp.sum(sq[hd * FOX_HEAD_DIM:(hd + 1) * FOX_HEAD_DIM, :], axis=0, keepdims=True)
                out = jnp.where(hid == hd, jnp.max(jnp.sqrt(n2), axis=1, keepdims=True), out)
            return out

        st_ref[0, 0, 0 * N_FOX_HEADS:1 * N_FOX_HEADS, :] = max_head_norms(qtb.astype(F32))
        st_ref[0, 0, 1 * N_FOX_HEADS:2 * N_FOX_HEADS, :] = max_head_norms(zkt.astype(BF16).astype(F32))
        st_ref[0, 0, 2 * N_FOX_HEADS:3 * N_FOX_HEADS, :] = jnp.broadcast_to(decay[:, 0:1], (N_FOX_HEADS, LANES))
        st_ref[0, 0, 3 * N_FOX_HEADS:4 * N_FOX_HEADS, :] = jnp.broadcast_to(decay[:, tm - 1:tm], (N_FOX_HEADS, LANES))

        rest = decay
        terms = []
        for _ in range(DECAY_TERMS):
            t = rest.astype(BF16).astype(F32)
            terms.append(t)
            rest = rest - t
        rid = lax.broadcasted_iota(jnp.int32, (DECAY_ROWS, tm), 0)
        for g in range(FOX_WIDTH // LANES):
            dec = jnp.zeros((DECAY_ROWS, tm), F32)
            for hh in range(HEADS_PER_STEP):
                hd = g * HEADS_PER_STEP + hh
                for ti, t in enumerate(terms):
                    dec = jnp.where(rid == hh * DECAY_TERMS + ti, t[hd:hd + 1, :], dec)
            dec = jnp.concatenate([dec, jnp.zeros((LANES - DECAY_ROWS, tm), F32)], axis=0)
            kx_ref[0, :, 2 * g * LANES:(2 * g + 1) * LANES] = zk[:, g * LANES:(g + 1) * LANES].astype(BF16)
            kx_ref[0, :, (2 * g + 1) * LANES:(2 * g + 2) * LANES] = dec.T.astype(BF16)
        for hd in range(N_FOX_HEADS):
            vxt_ref[0, hd * LANES:hd * LANES + FOX_HEAD_DIM, :] = (
                zvt[hd * FOX_HEAD_DIM:(hd + 1) * FOX_HEAD_DIM, :].astype(BF16))
            vxt_ref[0, hd * LANES + FOX_HEAD_DIM:(hd + 1) * LANES, :] = jnp.ones((LANES - FOX_HEAD_DIM, tm), BF16)
    else:
        ct_ref[0] = _lane_cumsum(lft, lane & (seg - 1), seg)

    z = jax.nn.gelu(zg)
    u = z[:, :GMLP_WIDTH]
    vn = _layer_norm(z[:, GMLP_WIDTH:], lng_ref[...], lnb_ref[...])
    if not prompt:
        gv_ref[0] = vn
    vnb = vn.astype(BF16)
    ri = lax.broadcasted_iota(jnp.int32, (chunk, chunk), 0)
    ci = lax.broadcasted_iota(jnp.int32, (chunk, chunk), 1)
    causal = _div_pow2(ci, STREAM_CHUNK) <= _div_pow2(ri, STREAM_CHUNK)
    wm = [jnp.where(causal, ws_ref[g], 0.0).astype(BF16) for g in range(GMLP_GROUPS)]
    for r0 in range(0, tm, chunk):
        mixed = jnp.concatenate(
            [_dot(wm[g], vnb[r0:r0 + chunk, g * GMLP_GROUP_DIM:(g + 1) * GMLP_GROUP_DIM])
             for g in range(GMLP_GROUPS)], axis=1) + bs_ref[...]
        gm = u[r0:r0 + chunk, :] * mixed
        hg_ref[0, r0:r0 + chunk, :] = _rms_norm(gm, gout_ref[...]).astype(BF16)


def _in_proj_mix(x, wq, wk, wv, wf, bfp, wg, lng, lnb, ws, bs_full, gout, *, tm, chunk, seg, prompt):
    G, R, D = x.shape
    grid = (G, R // tm)
    row = lambda w: pl.BlockSpec((1, tm, w), lambda g, j: (g, j, 0))
    full = lambda a: pl.BlockSpec(a.shape, lambda g, j: (0,) * a.ndim)
    in_specs = [row(D)] + [full(a) for a in (wq, wk, wv, wf, bfp, wg, lng, lnb, ws, bs_full, gout)]
    col = lambda h: pl.BlockSpec((1, h, tm), lambda g, j: (g, 0, j))
    wide = (lambda w: (G, w, R)) if prompt else (lambda w: (G, R, w))
    spec = col if prompt else row
    out_shape = [
        jax.ShapeDtypeStruct(wide(FOX_WIDTH), F32),
        jax.ShapeDtypeStruct(wide(FOX_WIDTH), F32),
        jax.ShapeDtypeStruct(wide(N_FOX_HEADS), F32),
        jax.ShapeDtypeStruct((G, R, GMLP_WIDTH), BF16),
    ]
    out_specs = [spec(FOX_WIDTH), spec(FOX_WIDTH), spec(N_FOX_HEADS), row(GMLP_WIDTH)]
    scratch = []
    if prompt:
        out_shape += [
            jax.ShapeDtypeStruct((G, FOX_WIDTH, R), BF16),
            jax.ShapeDtypeStruct((G, R, 2 * FOX_WIDTH), BF16),
            jax.ShapeDtypeStruct((G, N_FOX_HEADS * LANES, R), BF16),
            jax.ShapeDtypeStruct((G, R // tm, N_TILE_STATS * N_FOX_HEADS, LANES), F32)]
        out_specs += [col(FOX_WIDTH), row(2 * FOX_WIDTH), col(N_FOX_HEADS * LANES),
                      pl.BlockSpec((1, 1, N_TILE_STATS * N_FOX_HEADS, LANES), lambda g, j: (g, j, 0, 0))]
        scratch = [pltpu.VMEM((N_FOX_HEADS, tm), F32)]
    else:
        out_shape += [jax.ShapeDtypeStruct((G, R, FOX_WIDTH), BF16),
                      jax.ShapeDtypeStruct((G, N_FOX_HEADS, R), F32),
                      jax.ShapeDtypeStruct((G, R, GMLP_WIDTH), F32)]
        out_specs += [row(FOX_WIDTH), col(N_FOX_HEADS), row(GMLP_WIDTH)]
    return pl.pallas_call(
        functools.partial(_in_proj_kernel, tm=tm, chunk=chunk, seg=seg, prompt=prompt),
        grid=grid, in_specs=in_specs, out_specs=out_specs, out_shape=out_shape,
        scratch_shapes=scratch, compiler_params=_params("arbitrary", "arbitrary"),
        name="in_proj_mix_prompt" if prompt else "in_proj_mix_sample",
    )(x, wq, wk, wv, wf, bfp, wg, lng, lnb, ws, bs_full, gout)


def _fox_prompt_kernel(first_ref, qt_ref, kx_ref, vxt_ref, o_ref, qx_ref, m_ref, acc_ref, s_ref, bm_ref, *, tb):
    step = pl.program_id(2)
    flat = (pl.program_id(0) * pl.num_programs(1) + pl.program_id(1)) * pl.num_programs(2) + step
    dim = lax.broadcasted_iota(jnp.int32, (LANES, tb), 0)
    key = lax.broadcasted_iota(jnp.int32, (tb, tb), 0)
    qry = lax.broadcasted_iota(jnp.int32, (tb, tb), 1)
    causal = key <= qry
    for a in range(FOX_Q_BLOCKS):
        qt = qt_ref[0, :, a * tb:(a + 1) * tb]
        for h in range(HEADS_PER_STEP):
            own = _div_pow2(dim, FOX_HEAD_DIM) == h
            decay = (dim >= h * DECAY_TERMS) & (dim < (h + 1) * DECAY_TERMS)
            qx_ref[a, h, :LANES, :] = jnp.where(own, qt, jnp.zeros_like(qt))
            qx_ref[a, h, LANES:, :] = jnp.where(decay, 1.0, 0.0).astype(BF16)

    def reset():
        m_ref[...] = jnp.full_like(m_ref, -jnp.inf)
        acc_ref[...] = jnp.zeros_like(acc_ref)

    def scores(a, j):
        kx = kx_ref[0, pl.ds(pl.multiple_of(j * tb, tb), tb), :]
        return [_dot(kx, qx_ref[a, h]) for h in range(HEADS_PER_STEP)]

    def diagonal_scores(a, j):
        half = tb // 2
        kx = kx_ref[0, pl.ds(pl.multiple_of(j * tb, tb), tb), :]
        out = []
        for h in range(HEADS_PER_STEP):
            left = _dot(kx[:half], qx_ref[a, h, :, :half])
            left = jnp.concatenate([left, jnp.full((half, half), -jnp.inf, F32)], axis=0)
            right = _dot(kx, qx_ref[a, h, :, half:])
            out.append(jnp.where(causal, jnp.concatenate([left, right], axis=1), -jnp.inf))
        return out

    def stash(s_new):
        for h in range(HEADS_PER_STEP):
            s_ref[h] = s_new[h]
            bm_ref[h] = jnp.max(s_new[h], axis=0, keepdims=True)

    def consume(j):
        start = pl.multiple_of(j * tb, tb)
        for h in range(HEADS_PER_STEP):
            m_prev = m_ref[h]
            m_new = jnp.maximum(m_prev, bm_ref[h])
            p = jnp.exp2(s_ref[h] - m_new)
            pv = _dot(vxt_ref[0, h * LANES:(h + 1) * LANES, pl.ds(start, tb)], p.astype(BF16))
            acc_ref[h] = jnp.exp2(m_prev - m_new) * acc_ref[h] + pv
            m_ref[h] = m_new

    def finalize(a):
        outs = []
        for h in range(HEADS_PER_STEP):
            acc = acc_ref[h]
            outs.append(acc[:FOX_HEAD_DIM, :] / acc[FOX_HEAD_DIM:FOX_HEAD_DIM + 1, :])
        o_ref[0, a * tb:(a + 1) * tb, :] = jnp.concatenate(outs, axis=0).T

    reset()
    stash(diagonal_scores(0, step * FOX_Q_BLOCKS))
    for a in range(FOX_Q_BLOCKS):
        qi = step * FOX_Q_BLOCKS + a
        first = first_ref[flat * FOX_Q_BLOCKS + a]

        def pipelined(t, a=a, qi=qi, first=first):
            s_next = scores(a, t)
            consume(jnp.where(t == first, qi, t - 1))
            stash(s_next)

        def unrolled(t, carry, pipelined=pipelined, first=first):
            for u in range(KV_UNROLL):
                pipelined(first + KV_UNROLL * t + u)
            return carry

        def single(t, carry, pipelined=pipelined):
            pipelined(t)
            return carry

        def hand_over(last, a=a, qi=qi):
            if a + 1 < FOX_Q_BLOCKS:
                s_next = diagonal_scores(a + 1, qi + 1)
                consume(last)
                finalize(a)
                reset()
                stash(s_next)
            else:
                consume(last)
                finalize(a)

        n_live = qi - first

        @pl.when(n_live == KV_UNROLL)
        def _(pipelined=pipelined, hand_over=hand_over, first=first, qi=qi):
            for u in range(KV_UNROLL):
                pipelined(first + u)
            hand_over(qi - 1)

        @pl.when(n_live != KV_UNROLL)
        def _(unrolled=unrolled, single=single, hand_over=hand_over, first=first, qi=qi, n_live=n_live):
            n_unrolled = _div_pow2(n_live, KV_UNROLL)
            lax.fori_loop(0, n_unrolled, unrolled, 0)
            lax.fori_loop(first + n_unrolled * KV_UNROLL, qi, single, 0)
            hand_over(jnp.where(n_live > 0, qi - 1, qi))


def _first_live_block(stats):
    st = stats[..., 0]
    B, nblk, _ = st.shape
    qn, kn, d_first, d_last = (st[:, :, i * N_FOX_HEADS:(i + 1) * N_FOX_HEADS] for i in range(N_TILE_STATS))
    bound = (qn[:, :, None] * (kn[:, None, :] + kn[:, :, None])
             + d_last[:, None, :] - d_first[:, :, None])
    dead = bound < -PRUNE_LOG2_GAP
    dead = dead.reshape(B, nblk, nblk, N_FOX_HEADS // HEADS_PER_STEP, HEADS_PER_STEP).all(axis=-1)
    below = jnp.arange(nblk)[None, :] < jnp.arange(nblk)[:, None]
    dead = jnp.logical_and(dead, below[None, :, :, None])
    first = jnp.min(jnp.where(dead, nblk, jnp.arange(nblk)[None, None, :, None]), axis=2)
    return first.astype(jnp.int32).transpose(0, 2, 1).reshape(-1)


def _fox_prompt(qt, kx, vxt, stats, *, tb):
    B, _, S = qt.shape
    tq = FOX_Q_BLOCKS * tb
    assert stats.shape[1] == S // tb and S % tq == 0
    return pl.pallas_call(
        functools.partial(_fox_prompt_kernel, tb=tb),
        grid_spec=pltpu.PrefetchScalarGridSpec(
            num_scalar_prefetch=1,
            grid=(B, FOX_WIDTH // LANES, S // tq),
            in_specs=[
                pl.BlockSpec((1, LANES, tq), lambda b, g, i, first: (b, g, i)),
                pl.BlockSpec((1, S, 2 * LANES), lambda b, g, i, first: (b, 0, g)),
                pl.BlockSpec((1, 2 * LANES, S), lambda b, g, i, first: (b, g, 0)),
            ],
            out_specs=pl.BlockSpec((1, tq, LANES), lambda b, g, i, first: (b, i, g)),
            scratch_shapes=[pltpu.VMEM((FOX_Q_BLOCKS, HEADS_PER_STEP, 2 * LANES, tb), BF16),
                            pltpu.VMEM((HEADS_PER_STEP, 1, tb), F32),
                            pltpu.VMEM((HEADS_PER_STEP, LANES, tb), F32),
                            pltpu.VMEM((HEADS_PER_STEP, tb, tb), F32),
                            pltpu.VMEM((HEADS_PER_STEP, 1, tb), F32)]),
        out_shape=jax.ShapeDtypeStruct((B, S, FOX_WIDTH), F32),
        compiler_params=_params("arbitrary", "arbitrary", "arbitrary"),
        name="fox_prompt",
    )(_first_live_block(stats), qt, kx, vxt)


def _fox_sample_kernel(q_ref, kn_ref, vn_ref, cn_ref, ck_ref, cv_ref, lft_ref, o_ref, s_ref, *, P, T):
    rows = N_FOX_HEADS * T
    q = q_ref[0]
    qt = jnp.concatenate([q] * N_FOX_HEADS, axis=0)
    r = lax.broadcasted_iota(jnp.int32, (rows, FOX_WIDTH), 0)
    c = lax.broadcasted_iota(jnp.int32, (rows, FOX_WIDTH), 1)
    wt = jnp.where(_div_pow2(r, T) == _div_pow2(c, FOX_HEAD_DIM), qt, jnp.zeros_like(qt))

    lane = lax.broadcasted_iota(jnp.int32, (N_FOX_HEADS, P), 1)
    cs = _lane_cumsum(lft_ref[0], lane, P)
    suffix = cs[:, P - 1:P] - cs

    def head_rows(x):
        return jnp.concatenate(
            [jnp.broadcast_to(x[h:h + 1, :], (T, x.shape[1])) for h in range(N_FOX_HEADS)], axis=0)

    kc_w = SAMPLE_KEY_CHUNK
    for k0 in range(0, P, kc_w):
        kct = ck_ref[0, :, k0:k0 + kc_w].astype(BF16)
        s_ref[:, k0:k0 + kc_w] = _dot(wt, kct) + head_rows(suffix[:, k0:k0 + kc_w])
    pad = jnp.zeros((LANES - T, FOX_WIDTH), BF16)
    sn = _dot_nt(wt, jnp.concatenate([kn_ref[0].astype(BF16), pad], axis=0))
    rr = lax.broadcasted_iota(jnp.int32, (rows, LANES), 0)
    cc = lax.broadcasted_iota(jnp.int32, (rows, LANES), 1)
    s_ref[:, P:P + LANES] = jnp.where(cc <= (rr & (T - 1)), sn - head_rows(cn_ref[0]), -jnp.inf)

    n_cols = P + LANES
    m = jnp.full((rows, 1), -jnp.inf, F32)
    for k0 in range(0, n_cols, kc_w):
        w = min(kc_w, n_cols - k0)
        m = jnp.maximum(m, jnp.max(s_ref[:, k0:k0 + w], axis=1, keepdims=True))
    l = jnp.zeros((rows, 1), F32)
    acc = jnp.zeros((rows, FOX_WIDTH), F32)
    for k0 in range(0, P, kc_w):
        p = jnp.exp(s_ref[:, k0:k0 + kc_w] - m)
        l = l + jnp.sum(p, axis=1, keepdims=True)
        acc = acc + _dot_nt(p.astype(BF16), cv_ref[0, :, k0:k0 + kc_w].astype(BF16))
    p = jnp.exp(s_ref[:, P:P + LANES] - m)
    l = l + jnp.sum(p, axis=1, keepdims=True)
    acc = acc + _dot(p.astype(BF16), jnp.concatenate([vn_ref[0].astype(BF16), pad], axis=0))
    on = acc / l
    c16 = lax.broadcasted_iota(jnp.int32, (T, FOX_WIDTH), 1)
    out = jnp.zeros((T, FOX_WIDTH), F32)
    for h in range(N_FOX_HEADS):
        out = out + jnp.where(_div_pow2(c16, FOX_HEAD_DIM) == h, on[h * T:(h + 1) * T, :], 0.0)
    o_ref[0] = out


def _fox_sample(q, kn, vn, cn_pad, cache_k, cache_v, cache_lft):
    Bd, T, _ = q.shape
    P = cache_k.shape[2]
    per_b = lambda shape: pl.BlockSpec((1,) + shape, lambda b: (b, 0, 0))
    return pl.pallas_call(
        functools.partial(_fox_sample_kernel, P=P, T=T),
        grid=(Bd,),
        in_specs=[per_b((T, FOX_WIDTH)), per_b((T, FOX_WIDTH)), per_b((T, FOX_WIDTH)),
                  per_b((N_FOX_HEADS, LANES)), per_b((FOX_WIDTH, P)), per_b((FOX_WIDTH, P)),
                  per_b((N_FOX_HEADS, P))],
        out_specs=per_b((T, FOX_WIDTH)),
        out_shape=jax.ShapeDtypeStruct((Bd, T, FOX_WIDTH), F32),
        scratch_shapes=[pltpu.VMEM((N_FOX_HEADS * T, P + LANES), F32)],
        compiler_params=_params("arbitrary"),
        name="fox_sample",
    )(q, kn, vn, cn_pad, cache_k, cache_v, cache_lft)


def _memory_kv_kernel(mem_ref, wk_ref, wv_ref, k_ref, v_ref):
    mb = mem_ref[...].astype(BF16)
    k_ref[...] = _dot(mb, wk_ref[...])
    v_ref[...] = _dot(mb, wv_ref[...])


def _memory_kv(mem2d, wk, wv):
    n = mem2d.shape[0]
    full = lambda a: pl.BlockSpec(a.shape, lambda i: (0, 0))
    out = jax.ShapeDtypeStruct((n, D_MODEL), F32)
    return pl.pallas_call(
        _memory_kv_kernel, grid=(1,),
        in_specs=[full(mem2d), full(wk), full(wv)],
        out_specs=[pl.BlockSpec((n, D_MODEL), lambda i: (0, 0))] * 2,
        out_shape=[out, out], compiler_params=_params("arbitrary"), name="memory_kv",
    )(mem2d, wk, wv)


def _merge_xattn_kernel(x_ref, att_ref, hg_ref, mk_ref, mv_ref, gfox_ref, wo_ref, ln1g_ref, ln1b_ref,
                        wmq_ref, wmo_ref, ln2g_ref, ln2b_ref, o_ref, *, alpha):
    parts = _row_parts(x_ref.shape[1])
    head = lambda hh: slice(hh * MEM_HEAD_DIM, (hh + 1) * MEM_HEAD_DIM)
    h = [jnp.concatenate([_rms_norm(att_ref[0, r, :], gfox_ref[...]).astype(BF16), hg_ref[0, r, :]], axis=1)
         for r in parts]
    mix = [_dot(hi, wo_ref[...]) for hi in h]
    x1 = [_layer_norm(alpha * x_ref[0, r, :] + mi, ln1g_ref[...], ln1b_ref[...]) for r, mi in zip(parts, mix)]
    qm = [_dot(xi.astype(BF16), wmq_ref[...]) for xi in x1]
    qm = [(qi * (MEM_HEAD_DIM ** -0.5)).astype(BF16) for qi in qm]
    if len(mk_ref.shape) == 4:
        n_rows = N_MEM * N_MEM_HEADS
        mk = mk_ref[0].reshape(n_rows, MEM_HEAD_DIM).astype(BF16)
        mv = mv_ref[0].reshape(n_rows, MEM_HEAD_DIM).astype(BF16)
        own = lambda hh, sh: (lax.broadcasted_iota(jnp.int32, sh.shape, 1) & (N_MEM_HEADS - 1)) == hh
        s = [[_dot_nt(qi[:, head(hh)], mk) for hh in range(N_MEM_HEADS)] for qi in qm]
        s = [[jnp.where(own(hh, sh), sh, -jnp.inf) for hh, sh in enumerate(si)] for si in s]
        values = lambda hh: mv
    else:
        mk = mk_ref[0].astype(BF16)
        mv = mv_ref[0].astype(BF16)
        s = [[_dot_nt(qi[:, head(hh)], mk[:, head(hh)]) for hh in range(N_MEM_HEADS)] for qi in qm]
        values = lambda hh: mv[:, head(hh)]
    p = [[jnp.exp(sh - jnp.max(sh, axis=1, keepdims=True)) for sh in si] for si in s]
    o = [[_dot(ph.astype(BF16), values(hh)) / jnp.sum(ph, axis=1, keepdims=True)
          for hh, ph in enumerate(pi)] for pi in p]
    y = [_dot(jnp.concatenate(oi, axis=1).astype(BF16), wmo_ref[...]) for oi in o]
    for r, xi, yi in zip(parts, x1, y):
        o_ref[0, r, :] = _layer_norm(alpha * xi + yi, ln2g_ref[...], ln2b_ref[...])


def _merge_xattn(x, att, hg, mk, mv, gfox, wo, ln1g, ln1b, wmq, wmo, ln2g, ln2b, *, tm, alpha):
    G, R, D = x.shape
    row = lambda w: pl.BlockSpec((1, tm, w), lambda g, j: (g, j, 0))
    mem = pl.BlockSpec((1,) + mk.shape[1:], lambda g, j: (g,) + (0,) * (mk.ndim - 1))
    full = lambda a: pl.BlockSpec(a.shape, lambda g, j: (0,) * a.ndim)
    return pl.pallas_call(
        functools.partial(_merge_xattn_kernel, alpha=alpha),
        grid=(G, R // tm),
        in_specs=[row(D), row(FOX_WIDTH), row(GMLP_WIDTH), mem, mem] +
                 [full(a) for a in (gfox, wo, ln1g, ln1b, wmq, wmo, ln2g, ln2b)],
        out_specs=row(D), out_shape=jax.ShapeDtypeStruct((G, R, D), F32),
        compiler_params=_params("arbitrary", "arbitrary"), name=f"merge_xattn_{tm}",
    )(x, att, hg, mk, mv, gfox, wo, ln1g, ln1b, wmq, wmo, ln2g, ln2b)


def _ffn_kernel(x_ref, wg_ref, wu_ref, wd_ref, g_ref, b_ref, o_ref, h_ref, *, alpha, d_ff):
    parts = _row_parts(x_ref.shape[1])
    xb = [x_ref[0, r, :].astype(BF16) for r in parts]
    for c0 in range(0, d_ff, FFN_COL_CHUNK):
        cols = slice(c0, min(c0 + FFN_COL_CHUNK, d_ff))
        for r, xi in zip(parts, xb):
            gate = _dot(xi, wg_ref[:, cols])
            up = _dot(xi, wu_ref[:, cols])
            h_ref[r, cols] = (jax.nn.silu(gate) * up).astype(BF16)
    y = [_dot(h_ref[r, :], wd_ref[...]) for r in parts]
    for r, yi in zip(parts, y):
        o_ref[0, r, :] = _layer_norm(alpha * x_ref[0, r, :] + yi, g_ref[...], b_ref[...])


def _ffn(x, wg, wu, wd, g, b, *, tm, alpha):
    G, R, D = x.shape
    d_ff = wg.shape[1]
    assert d_ff % LANES == 0
    row = pl.BlockSpec((1, tm, D), lambda gi, j: (gi, j, 0))
    full = lambda a: pl.BlockSpec(a.shape, lambda gi, j: (0,) * a.ndim, pipeline_mode=pl.Buffered(1))
    return pl.pallas_call(
        functools.partial(_ffn_kernel, alpha=alpha, d_ff=d_ff),
        grid=(G, R // tm),
        in_specs=[row] + [full(a) for a in (wg, wu, wd, g, b)],
        out_specs=row, out_shape=jax.ShapeDtypeStruct((G, R, D), F32),
        scratch_shapes=[pltpu.VMEM((tm, d_ff), BF16)],
        compiler_params=_params("arbitrary", "arbitrary"), name=f"ffn_{tm}",
    )(x, wg, wu, wd, g, b)


def _row(a):
    return a.reshape(1, -1)


def kernel(x_prompt, x_sample, cache_fox_k, cache_fox_v, cache_fox_logf, cache_mem_k, cache_mem_v, mem_prompt, w_in, b_f, g_fox_out, g_gmlp_out, sgu_ln_g, sgu_ln_b, w_s, b_s, w_o, ln1_g, ln1_b, w_mq, w_mk, w_mv, w_mo, ln2_g, ln2_b, w_gate, w_up, w_down, ln3_g, ln3_b):
    depth = w_in.shape[0]
    B, S, D = x_prompt.shape
    Bd, T, _ = x_sample.shape
    P = cache_fox_k.shape[2]
    alpha = (2.0 * depth) ** 0.25
    assert D == D_MODEL and S % FOX_BLOCK == 0 and S % PROMPT_ROW_TILE == 0
    assert PROMPT_ROW_TILE % GMLP_CHUNK == 0 and (T & (T - 1)) == 0 and N_FOX_HEADS * T == LANES
    assert P % SAMPLE_KEY_CHUNK == 0 and S % TAIL_ROW_TILE == 0 and S % FFN_ROW_TILE == 0

    yp, ys = x_prompt, x_sample
    outs = [[] for _ in range(9)]
    for l in range(depth):
        w = w_in[l].astype(BF16)
        wq, wk, wv = (w[:, i * FOX_WIDTH:(i + 1) * FOX_WIDTH] for i in range(3))
        f0 = 3 * FOX_WIDTH
        wf = jnp.pad(w[:, f0:f0 + N_FOX_HEADS], ((0, 0), (0, LANES - N_FOX_HEADS)))
        bfp = jnp.pad(_row(b_f[l]), ((0, 0), (0, LANES - N_FOX_HEADS)))
        wg = w[:, f0 + N_FOX_HEADS:]
        lng, lnb, gout = _row(sgu_ln_g[l]), _row(sgu_ln_b[l]), _row(g_gmlp_out[l])
        tail_w = (_row(g_fox_out[l]), w_o[l].astype(BF16), _row(ln1_g[l]), _row(ln1_b[l]),
                  w_mq[l].astype(BF16), w_mo[l].astype(BF16), _row(ln2_g[l]), _row(ln2_b[l]))
        ffn_w = (w_gate[l].astype(BF16), w_up[l].astype(BF16), w_down[l].astype(BF16),
                 _row(ln3_g[l]), _row(ln3_b[l]))

        def mix_params(n):
            bias = jnp.repeat(b_s[l][:, :n].T, GMLP_GROUP_DIM, axis=1)
            return w_s[l][:, :n, :n], bias

        ws_p, bs_p = mix_params(GMLP_CHUNK)
        k, v, lf, hg, qt, kx, vxt, stats = _in_proj_mix(
            yp, wq, wk, wv, wf, bfp, wg, lng, lnb, ws_p, bs_p, gout,
            tm=PROMPT_ROW_TILE, chunk=GMLP_CHUNK, seg=None, prompt=True)
        att = _fox_prompt(qt, kx, vxt, stats, tb=FOX_BLOCK)
        mk, mv = _memory_kv(mem_prompt.reshape(B * N_MEM, D), w_mk[l].astype(BF16), w_mv[l].astype(BF16))
        mk, mv = mk.reshape(B, N_MEM, D), mv.reshape(B, N_MEM, D)
        x2 = _merge_xattn(yp, att, hg, mk, mv, *tail_w, tm=TAIL_ROW_TILE, alpha=alpha)
        yp = _ffn(x2, *ffn_w, tm=FFN_ROW_TILE, alpha=alpha)
        heads_last = lambda a: a.reshape(B, N_FOX_HEADS, FOX_HEAD_DIM, S).transpose(0, 3, 1, 2)
        outs[0].append(heads_last(k))
        outs[1].append(heads_last(v))
        outs[2].append(lf.transpose(0, 2, 1))
        outs[3].append(mk.reshape(B, N_MEM, N_MEM_HEADS, MEM_HEAD_DIM))
        outs[4].append(mv.reshape(B, N_MEM, N_MEM_HEADS, MEM_HEAD_DIM))

        ws_s, bs_s = mix_params(T)
        k, v, lf, hg, q, ct, gv = _in_proj_mix(
            ys.reshape(1, Bd * T, D), wq, wk, wv, wf, bfp, wg, lng, lnb, ws_s, bs_s, gout,
            tm=Bd * T, chunk=T, seg=T, prompt=False)
        per_b = lambda a: a.reshape(Bd, T, a.shape[-1])
        cn = ct.reshape(N_FOX_HEADS, Bd, T).transpose(1, 0, 2)
        cn = jnp.pad(cn, ((0, 0), (0, 0), (0, LANES - T)))
        dims_first = lambda a: a.transpose(0, 2, 3, 1).reshape(Bd, FOX_WIDTH, P)
        att = _fox_sample(per_b(q), per_b(k), per_b(v), cn,
                          dims_first(cache_fox_k[l]), dims_first(cache_fox_v[l]),
                          jnp.swapaxes(cache_fox_logf[l], 1, 2))
        x2 = _merge_xattn(ys, att, per_b(hg), cache_mem_k[l], cache_mem_v[l], *tail_w, tm=T, alpha=alpha)
        ys = _ffn(x2.reshape(1, Bd * T, D), *ffn_w, tm=Bd * T, alpha=alpha).reshape(Bd, T, D)
        outs[5].append(k.reshape(Bd, T, N_FOX_HEADS, FOX_HEAD_DIM))
        outs[6].append(v.reshape(Bd, T, N_FOX_HEADS, FOX_HEAD_DIM))
        outs[7].append(per_b(lf))
        outs[8].append(per_b(gv))
    return (yp, ys) + tuple(jnp.stack(o) for o in outs)
```

```python
import functools

import jax
import jax.numpy as jnp
from jax import lax
from jax.experimental import pallas as pl
from jax.experimental.pallas import tpu as pltpu

F32 = jnp.float32
BF16 = jnp.bfloat16

LANES = 128
D_MODEL = 1024
STREAM_CHUNK = 64
N_FOX_HEADS = 8
FOX_HEAD_DIM = 64
FOX_WIDTH = N_FOX_HEADS * FOX_HEAD_DIM
HEADS_PER_STEP = LANES // FOX_HEAD_DIM
GMLP_WIDTH = D_MODEL - FOX_WIDTH
GMLP_GROUPS = 4
GMLP_GROUP_DIM = GMLP_WIDTH // GMLP_GROUPS
GMLP_CHUNK = 128
N_MEM = 256
N_MEM_HEADS = 4
MEM_HEAD_DIM = D_MODEL // N_MEM_HEADS
EPS = 1e-5
LOG2E = 1.4426950408889634
DECAY_TERMS = 3
DECAY_ROWS = 16
assert HEADS_PER_STEP * DECAY_TERMS <= DECAY_ROWS
N_TILE_STATS = 4
PRUNE_LOG2_GAP = 160.0
FIXED_SHIFT_MAX_LOG2 = 64.0
VMEM_LIMIT = 56 * 1024 * 1024

PROMPT_ROW_TILE = 512
FOX_BLOCK = 512
KV_UNROLL = 4
FOX_Q_BLOCKS = 4
TAIL_ROW_TILE = 1024
TAIL_PART_ROWS = 256
FFN_ROW_TILE = 1024
FFN_COL_CHUNK = 512
SAMPLE_KEY_CHUNK = 512


def _dot(a, b):
    return jnp.dot(a, b, preferred_element_type=F32)


def _dot_nt(a, b):
    return lax.dot_general(a, b, (((1,), (1,)), ((), ())), preferred_element_type=F32)


def _layer_norm(x, g, b):
    mu = jnp.mean(x, axis=-1, keepdims=True)
    xc = x - mu
    var = jnp.mean(xc * xc, axis=-1, keepdims=True)
    return xc * lax.rsqrt(var + EPS) * g + b


def _rms_norm(x, g):
    return x * lax.rsqrt(jnp.mean(x * x, axis=-1, keepdims=True) + EPS) * g


def _lane_cumsum(x, pos, length):
    shift = 1
    while shift < length:
        x = x + jnp.where(pos >= shift, pltpu.roll(x, shift, axis=1), 0.0)
        shift *= 2
    return x


def _div_pow2(x, n):
    assert n & (n - 1) == 0
    return x >> (n.bit_length() - 1)


def _row_parts(tm):
    n = tm // TAIL_PART_ROWS if tm % TAIL_PART_ROWS == 0 else 1
    return [slice(i * tm // n, (i + 1) * tm // n) for i in range(n)]


def _params(*semantics):
    return pltpu.CompilerParams(dimension_semantics=semantics, vmem_limit_bytes=VMEM_LIMIT)


def _in_proj_kernel(*refs, tm, chunk, seg, prompt):
    (x_ref, wq_ref, wk_ref, wv_ref, wf_ref, bf_ref, wg_ref, lng_ref, lnb_ref,
     ws_ref, bs_ref, gout_ref) = refs[:12]
    if prompt:
        k_ref, v_ref, lf_ref, hg_ref, qt_ref, kx_ref, vxt_ref, st_ref, carry_ref = refs[12:]
    else:
        k_ref, v_ref, lf_ref, hg_ref, q_ref, ct_ref, gv_ref = refs[12:]

    if prompt:
        @pl.when(pl.program_id(1) == 0)
        def _():
            carry_ref[...] = jnp.zeros_like(carry_ref)

    xb = x_ref[0].astype(BF16)
    zf = _dot(xb, wf_ref[...]) + bf_ref[...]
    zg = _dot(xb, wg_ref[...])
    zq = _dot(xb, wq_ref[...])
    if prompt:
        qtb = (zq * (FOX_HEAD_DIM ** -0.5 * LOG2E)).T.astype(BF16)
        qt_ref[0] = qtb
    else:
        q_ref[0] = (zq * (FOX_HEAD_DIM ** -0.5)).astype(BF16)
    zk = _dot(xb, wk_ref[...])
    zv = _dot(xb, wv_ref[...])
    lf = jax.nn.log_sigmoid(zf)
    lft = lf.T[:N_FOX_HEADS, :]
    if prompt:
        zvt = zv.T
        zkt = zk.T
        k_ref[0] = zkt
        v_ref[0] = zvt
        lf_ref[0] = lft
    else:
        k_ref[0] = zk
        v_ref[0] = zv
        lf_ref[0] = lf[:, :N_FOX_HEADS]
    lane = lax.broadcasted_iota(jnp.int32, (N_FOX_HEADS, tm), 1)
    if prompt:
        c = _lane_cumsum(lft, lane, tm) + carry_ref[...]
        carry_ref[...] = jnp.broadcast_to(c[:, tm - 1:tm], (N_FOX_HEADS, tm))

        decay = c * (-LOG2E)

        def max_head_norms(xt):
            sq = xt * xt
            hid = lax.broadcasted_iota(jnp.int32, (N_FOX_HEADS, LANES), 0)
            out = jnp.zeros((N_FOX_HEADS, LANES), F32)
            for hd in range(N_FOX_HEADS):
                n2 = jnp.sum(sq[hd * FOX_HEAD_DIM:(hd + 1) * FOX_HEAD_DIM, :], axis=0, keepdims=True)
                out = jnp.where(hid == hd, jnp.max(jnp.sqrt(n2), axis=1, keepdims=True), out)
            return out

        st_ref[0, 0, 0 * N_FOX_HEADS:1 * N_FOX_HEADS, :] = max_head_norms(qtb.astype(F32))
        st_ref[0, 0, 1 * N_FOX_HEADS:2 * N_FOX_HEADS, :] = max_head_norms(zkt.astype(BF16).astype(F32))
        st_ref[0, 0, 2 * N_FOX_HEADS:3 * N_FOX_HEADS, :] = jnp.broadcast_to(decay[:, 0:1], (N_FOX_HEADS, LANES))
        st_ref[0, 0, 3 * N_FOX_HEADS:4 * N_FOX_HEADS, :] = jnp.broadcast_to(decay[:, tm - 1:tm], (N_FOX_HEADS, LANES))

        rest = decay
        terms = []
        for _ in range(DECAY_TERMS):
            t = rest.astype(BF16).astype(F32)
            terms.append(t)
            rest = rest - t
        rid = lax.broadcasted_iota(jnp.int32, (DECAY_ROWS, tm), 0)
        for g in range(FOX_WIDTH // LANES):
            dec = jnp.zeros((DECAY_ROWS, tm), F32)
            for hh in range(HEADS_PER_STEP):
                hd = g * HEADS_PER_STEP + hh
                for ti, t in enumerate(terms):
                    dec = jnp.where(rid == hh * DECAY_TERMS + ti, t[hd:hd + 1, :], dec)
            dec = jnp.concatenate([dec, jnp.zeros((LANES - DECAY_ROWS, tm), F32)], axis=0)
            kx_ref[0, :, 2 * g * LANES:(2 * g + 1) * LANES] = zk[:, g * LANES:(g + 1) * LANES].astype(BF16)
            kx_ref[0, :, (2 * g + 1) * LANES:(2 * g + 2) * LANES] = dec.T.astype(BF16)
        for hd in range(N_FOX_HEADS):
            vxt_ref[0, hd * LANES:hd * LANES + FOX_HEAD_DIM, :] = (
                zvt[hd * FOX_HEAD_DIM:(hd + 1) * FOX_HEAD_DIM, :].astype(BF16))
            vxt_ref[0, hd * LANES + FOX_HEAD_DIM:(hd + 1) * LANES, :] = jnp.ones((LANES - FOX_HEAD_DIM, tm), BF16)
    else:
        ct_ref[0] = _lane_cumsum(lft, lane & (seg - 1), seg)

    z = jax.nn.gelu(zg)
    u = z[:, :GMLP_WIDTH]
    vn = _layer_norm(z[:, GMLP_WIDTH:], lng_ref[...], lnb_ref[...])
    if not prompt:
        gv_ref[0] = vn
    vnb = vn.astype(BF16)
    ri = lax.broadcasted_iota(jnp.int32, (chunk, chunk), 0)
    ci = lax.broadcasted_iota(jnp.int32, (chunk, chunk), 1)
    causal = _div_pow2(ci, STREAM_CHUNK) <= _div_pow2(ri, STREAM_CHUNK)
    wm = [jnp.where(causal, ws_ref[g], 0.0).astype(BF16) for g in range(GMLP_GROUPS)]
    for r0 in range(0, tm, chunk):
        mixed = jnp.concatenate(
            [_dot(wm[g], vnb[r0:r0 + chunk, g * GMLP_GROUP_DIM:(g + 1) * GMLP_GROUP_DIM])
             for g in range(GMLP_GROUPS)], axis=1) + bs_ref[...]
        gm = u[r0:r0 + chunk, :] * mixed
        hg_ref[0, r0:r0 + chunk, :] = _rms_norm(gm, gout_ref[...]).astype(BF16)


def _in_proj_mix(x, wq, wk, wv, wf, bfp, wg, lng, lnb, ws, bs_full, gout, *, tm, chunk, seg, prompt):
    G, R, D = x.shape
    grid = (G, R // tm)
    row = lambda w: pl.BlockSpec((1, tm, w), lambda g, j: (g, j, 0))
    full = lambda a: pl.BlockSpec(a.shape, lambda g, j: (0,) * a.ndim)
    in_specs = [row(D)] + [full(a) for a in (wq, wk, wv, wf, bfp, wg, lng, lnb, ws, bs_full, gout)]
    col = lambda h: pl.BlockSpec((1, h, tm), lambda g, j: (g, 0, j))
    wide = (lambda w: (G, w, R)) if prompt else (lambda w: (G, R, w))
    spec = col if prompt else row
    out_shape = [
        jax.ShapeDtypeStruct(wide(FOX_WIDTH), F32),
        jax.ShapeDtypeStruct(wide(FOX_WIDTH), F32),
        jax.ShapeDtypeStruct(wide(N_FOX_HEADS), F32),
        jax.ShapeDtypeStruct((G, R, GMLP_WIDTH), BF16),
    ]
    out_specs = [spec(FOX_WIDTH), spec(FOX_WIDTH), spec(N_FOX_HEADS), row(GMLP_WIDTH)]
    scratch = []
    if prompt:
        out_shape += [
            jax.ShapeDtypeStruct((G, FOX_WIDTH, R), BF16),
            jax.ShapeDtypeStruct((G, R, 2 * FOX_WIDTH), BF16),
            jax.ShapeDtypeStruct((G, N_FOX_HEADS * LANES, R), BF16),
            jax.ShapeDtypeStruct((G, R // tm, N_TILE_STATS * N_FOX_HEADS, LANES), F32)]
        out_specs += [col(FOX_WIDTH), row(2 * FOX_WIDTH), col(N_FOX_HEADS * LANES),
                      pl.BlockSpec((1, 1, N_TILE_STATS * N_FOX_HEADS, LANES), lambda g, j: (g, j, 0, 0))]
        scratch = [pltpu.VMEM((N_FOX_HEADS, tm), F32)]
    else:
        out_shape += [jax.ShapeDtypeStruct((G, R, FOX_WIDTH), BF16),
                      jax.ShapeDtypeStruct((G, N_FOX_HEADS, R), F32),
                      jax.ShapeDtypeStruct((G, R, GMLP_WIDTH), F32)]
        out_specs += [row(FOX_WIDTH), col(N_FOX_HEADS), row(GMLP_WIDTH)]
    return pl.pallas_call(
        functools.partial(_in_proj_kernel, tm=tm, chunk=chunk, seg=seg, prompt=prompt),
        grid=grid, in_specs=in_specs, out_specs=out_specs, out_shape=out_shape,
        scratch_shapes=scratch, compiler_params=_params("arbitrary", "arbitrary"),
        name="in_proj_mix_prompt" if prompt else "in_proj_mix_sample",
    )(x, wq, wk, wv, wf, bfp, wg, lng, lnb, ws, bs_full, gout)


def _fox_prompt_kernel(first_ref, qt_ref, kx_ref, vxt_ref, o_ref, qx_ref, m_ref, acc_ref, s_ref, bm_ref, *,
                       tb, fixed_shift):
    step = pl.program_id(2)
    flat = (pl.program_id(0) * pl.num_programs(1) + pl.program_id(1)) * pl.num_programs(2) + step
    dim = lax.broadcasted_iota(jnp.int32, (LANES, tb), 0)
    key = lax.broadcasted_iota(jnp.int32, (tb, tb), 0)
    qry = lax.broadcasted_iota(jnp.int32, (tb, tb), 1)
    causal = key <= qry
    for a in range(FOX_Q_BLOCKS):
        qt = qt_ref[0, :, a * tb:(a + 1) * tb]
        for h in range(HEADS_PER_STEP):
            own = _div_pow2(dim, FOX_HEAD_DIM) == h
            decay = (dim >= h * DECAY_TERMS) & (dim < (h + 1) * DECAY_TERMS)
            qx_ref[a, h, :LANES, :] = jnp.where(own, qt, jnp.zeros_like(qt))
            qx_ref[a, h, LANES:, :] = jnp.where(decay, 1.0, 0.0).astype(BF16)

    def reset():
        m_ref[...] = jnp.full_like(m_ref, -jnp.inf)
        acc_ref[...] = jnp.zeros_like(acc_ref)

    def scores(a, j):
        kx = kx_ref[0, pl.ds(pl.multiple_of(j * tb, tb), tb), :]
        return [_dot(kx, qx_ref[a, h]) for h in range(HEADS_PER_STEP)]

    def diagonal_scores(a, j):
        half = tb // 2
        kx = kx_ref[0, pl.ds(pl.multiple_of(j * tb, tb), tb), :]
        out = []
        for h in range(HEADS_PER_STEP):
            left = _dot(kx[:half], qx_ref[a, h, :, :half])
            left = jnp.concatenate([left, jnp.full((half, half), -jnp.inf, F32)], axis=0)
            right = _dot(kx, qx_ref[a, h, :, half:])
            out.append(jnp.where(causal, jnp.concatenate([left, right], axis=1), -jnp.inf))
        return out

    def stash(s_new, diagonal=False):
        for h in range(HEADS_PER_STEP):
            s_ref[h] = s_new[h]
            if not fixed_shift:
                bm_ref[h] = jnp.max(s_new[h], axis=0, keepdims=True)
            elif diagonal:
                m_ref[h] = jnp.max(s_new[h], axis=0, keepdims=True)

    def consume(j):
        start = pl.multiple_of(j * tb, tb)
        for h in range(HEADS_PER_STEP):
            vt = vxt_ref[0, h * LANES:(h + 1) * LANES, pl.ds(start, tb)]
            if fixed_shift:
                p = jnp.exp2(s_ref[h] - m_ref[h])
                acc_ref[h] = acc_ref[h] + _dot(vt, p.astype(BF16))
            else:
                m_prev = m_ref[h]
                m_new = jnp.maximum(m_prev, bm_ref[h])
                p = jnp.exp2(s_ref[h] - m_new)
                acc_ref[h] = jnp.exp2(m_prev - m_new) * acc_ref[h] + _dot(vt, p.astype(BF16))
                m_ref[h] = m_new

    def finalize(a):
        outs = []
        for h in range(HEADS_PER_STEP):
            acc = acc_ref[h]
            outs.append(acc[:FOX_HEAD_DIM, :] / acc[FOX_HEAD_DIM:FOX_HEAD_DIM + 1, :])
        o_ref[0, a * tb:(a + 1) * tb, :] = jnp.concatenate(outs, axis=0).T

    reset()
    stash(diagonal_scores(0, step * FOX_Q_BLOCKS), diagonal=True)
    for a in range(FOX_Q_BLOCKS):
        qi = step * FOX_Q_BLOCKS + a
        first = first_ref[flat * FOX_Q_BLOCKS + a]

        def pipelined(t, a=a, qi=qi, first=first):
            s_next = scores(a, t)
            consume(jnp.where(t == first, qi, t - 1))
            stash(s_next)

        def unrolled(t, carry, pipelined=pipelined, first=first):
            for u in range(KV_UNROLL):
                pipelined(first + KV_UNROLL * t + u)
            return carry

        def single(t, carry, pipelined=pipelined):
            pipelined(t)
            return carry

        def hand_over(last, a=a, qi=qi):
            if a + 1 < FOX_Q_BLOCKS:
                s_next = diagonal_scores(a + 1, qi + 1)
                consume(last)
                finalize(a)
                reset()
                stash(s_next, diagonal=True)
            else:
                consume(last)
                finalize(a)

        n_live = qi - first

        @pl.when(n_live == KV_UNROLL)
        def _(pipelined=pipelined, hand_over=hand_over, first=first, qi=qi):
            for u in range(KV_UNROLL):
                pipelined(first + u)
            hand_over(qi - 1)

        @pl.when(n_live != KV_UNROLL)
        def _(unrolled=unrolled, single=single, hand_over=hand_over, first=first, qi=qi, n_live=n_live):
            n_unrolled = _div_pow2(n_live, KV_UNROLL)
            lax.fori_loop(0, n_unrolled, unrolled, 0)
            lax.fori_loop(first + n_unrolled * KV_UNROLL, qi, single, 0)
            hand_over(jnp.where(n_live > 0, qi - 1, qi))


def _first_live_block(stats):
    st = stats[..., 0]
    B, nblk, _ = st.shape
    qn, kn, d_first, d_last = (st[:, :, i * N_FOX_HEADS:(i + 1) * N_FOX_HEADS] for i in range(N_TILE_STATS))
    bound = (qn[:, :, None] * (kn[:, None, :] + kn[:, :, None])
             + d_last[:, None, :] - d_first[:, :, None])
    dead = bound < -PRUNE_LOG2_GAP
    dead = dead.reshape(B, nblk, nblk, N_FOX_HEADS // HEADS_PER_STEP, HEADS_PER_STEP).all(axis=-1)
    below = jnp.arange(nblk)[None, :] < jnp.arange(nblk)[:, None]
    dead = jnp.logical_and(dead, below[None, :, :, None])
    first = jnp.min(jnp.where(dead, nblk, jnp.arange(nblk)[None, None, :, None]), axis=2)
    return first.astype(jnp.int32).transpose(0, 2, 1).reshape(-1)


def _fixed_shift_is_safe(stats):
    st = stats[..., 0]
    qn, kn = st[:, :, :N_FOX_HEADS], st[:, :, N_FOX_HEADS:2 * N_FOX_HEADS]
    return jnp.max(jnp.max(qn, axis=1) * 2.0 * jnp.max(kn, axis=1)) < FIXED_SHIFT_MAX_LOG2


def _fox_prompt(qt, kx, vxt, stats, *, tb):
    first = _first_live_block(stats)
    return lax.cond(_fixed_shift_is_safe(stats),
                    functools.partial(_fox_prompt_call, tb=tb, fixed_shift=True),
                    functools.partial(_fox_prompt_call, tb=tb, fixed_shift=False),
                    first, qt, kx, vxt)


def _fox_prompt_call(first, qt, kx, vxt, *, tb, fixed_shift):
    B, _, S = qt.shape
    tq = FOX_Q_BLOCKS * tb
    assert first.shape[0] == B * (FOX_WIDTH // LANES) * (S // tb) and S % tq == 0
    return pl.pallas_call(
        functools.partial(_fox_prompt_kernel, tb=tb, fixed_shift=fixed_shift),
        grid_spec=pltpu.PrefetchScalarGridSpec(
            num_scalar_prefetch=1,
            grid=(B, FOX_WIDTH // LANES, S // tq),
            in_specs=[
                pl.BlockSpec((1, LANES, tq), lambda b, g, i, first: (b, g, i)),
                pl.BlockSpec((1, S, 2 * LANES), lambda b, g, i, first: (b, 0, g)),
                pl.BlockSpec((1, 2 * LANES, S), lambda b, g, i, first: (b, g, 0)),
            ],
            out_specs=pl.BlockSpec((1, tq, LANES), lambda b, g, i, first: (b, i, g)),
            scratch_shapes=[pltpu.VMEM((FOX_Q_BLOCKS, HEADS_PER_STEP, 2 * LANES, tb), BF16),
                            pltpu.VMEM((HEADS_PER_STEP, 1, tb), F32),
                            pltpu.VMEM((HEADS_PER_STEP, LANES, tb), F32),
                            pltpu.VMEM((HEADS_PER_STEP, tb, tb), F32),
                            pltpu.VMEM((HEADS_PER_STEP, 1, tb), F32)]),
        out_shape=jax.ShapeDtypeStruct((B, S, FOX_WIDTH), F32),
        compiler_params=_params("arbitrary", "arbitrary", "arbitrary"),
        name="fox_prompt_fixed_shift" if fixed_shift else "fox_prompt",
    )(first, qt, kx, vxt)


def _fox_sample_kernel(q_ref, kn_ref, vn_ref, cn_ref, ck_ref, cv_ref, lft_ref, o_ref, s_ref, *, P, T):
    rows = N_FOX_HEADS * T
    q = q_ref[0]
    qt = jnp.concatenate([q] * N_FOX_HEADS, axis=0)
    r = lax.broadcasted_iota(jnp.int32, (rows, FOX_WIDTH), 0)
    c = lax.broadcasted_iota(jnp.int32, (rows, FOX_WIDTH), 1)
    wt = jnp.where(_div_pow2(r, T) == _div_pow2(c, FOX_HEAD_DIM), qt, jnp.zeros_like(qt))

    lane = lax.broadcasted_iota(jnp.int32, (N_FOX_HEADS, P), 1)
    cs = _lane_cumsum(lft_ref[0], lane, P)
    suffix = cs[:, P - 1:P] - cs

    def head_rows(x):
        return jnp.concatenate(
            [jnp.broadcast_to(x[h:h + 1, :], (T, x.shape[1])) for h in range(N_FOX_HEADS)], axis=0)

    kc_w = SAMPLE_KEY_CHUNK
    for k0 in range(0, P, kc_w):
        kct = ck_ref[0, :, k0:k0 + kc_w].astype(BF16)
        s_ref[:, k0:k0 + kc_w] = _dot(wt, kct) + head_rows(suffix[:, k0:k0 + kc_w])
    pad = jnp.zeros((LANES - T, FOX_WIDTH), BF16)
    sn = _dot_nt(wt, jnp.concatenate([kn_ref[0].astype(BF16), pad], axis=0))
    rr = lax.broadcasted_iota(jnp.int32, (rows, LANES), 0)
    cc = lax.broadcasted_iota(jnp.int32, (rows, LANES), 1)
    s_ref[:, P:P + LANES] = jnp.where(cc <= (rr & (T - 1)), sn - head_rows(cn_ref[0]), -jnp.inf)

    n_cols = P + LANES
    m = jnp.full((rows, 1), -jnp.inf, F32)
    for k0 in range(0, n_cols, kc_w):
        w = min(kc_w, n_cols - k0)
        m = jnp.maximum(m, jnp.max(s_ref[:, k0:k0 + w], axis=1, keepdims=True))
    l = jnp.zeros((rows, 1), F32)
    acc = jnp.zeros((rows, FOX_WIDTH), F32)
    for k0 in range(0, P, kc_w):
        p = jnp.exp(s_ref[:, k0:k0 + kc_w] - m)
        l = l + jnp.sum(p, axis=1, keepdims=True)
        acc = acc + _dot_nt(p.astype(BF16), cv_ref[0, :, k0:k0 + kc_w].astype(BF16))
    p = jnp.exp(s_ref[:, P:P + LANES] - m)
    l = l + jnp.sum(p, axis=1, keepdims=True)
    acc = acc + _dot(p.astype(BF16), jnp.concatenate([vn_ref[0].astype(BF16), pad], axis=0))
    on = acc / l
    c16 = lax.broadcasted_iota(jnp.int32, (T, FOX_WIDTH), 1)
    out = jnp.zeros((T, FOX_WIDTH), F32)
    for h in range(N_FOX_HEADS):
        out = out + jnp.where(_div_pow2(c16, FOX_HEAD_DIM) == h, on[h * T:(h + 1) * T, :], 0.0)
    o_ref[0] = out


def _fox_sample(q, kn, vn, cn_pad, cache_k, cache_v, cache_lft):
    Bd, T, _ = q.shape
    P = cache_k.shape[2]
    per_b = lambda shape: pl.BlockSpec((1,) + shape, lambda b: (b, 0, 0))
    return pl.pallas_call(
        functools.partial(_fox_sample_kernel, P=P, T=T),
        grid=(Bd,),
        in_specs=[per_b((T, FOX_WIDTH)), per_b((T, FOX_WIDTH)), per_b((T, FOX_WIDTH)),
                  per_b((N_FOX_HEADS, LANES)), per_b((FOX_WIDTH, P)), per_b((FOX_WIDTH, P)),
                  per_b((N_FOX_HEADS, P))],
        out_specs=per_b((T, FOX_WIDTH)),
        out_shape=jax.ShapeDtypeStruct((Bd, T, FOX_WIDTH), F32),
        scratch_shapes=[pltpu.VMEM((N_FOX_HEADS * T, P + LANES), F32)],
        compiler_params=_params("arbitrary"),
        name="fox_sample",
    )(q, kn, vn, cn_pad, cache_k, cache_v, cache_lft)


def _memory_kv_kernel(mem_ref, wk_ref, wv_ref, k_ref, v_ref):
    mb = mem_ref[...].astype(BF16)
    k_ref[...] = _dot(mb, wk_ref[...])
    v_ref[...] = _dot(mb, wv_ref[...])


def _memory_kv(mem2d, wk, wv):
    n = mem2d.shape[0]
    full = lambda a: pl.BlockSpec(a.shape, lambda i: (0, 0))
    out = jax.ShapeDtypeStruct((n, D_MODEL), F32)
    return pl.pallas_call(
        _memory_kv_kernel, grid=(1,),
        in_specs=[full(mem2d), full(wk), full(wv)],
        out_specs=[pl.BlockSpec((n, D_MODEL), lambda i: (0, 0))] * 2,
        out_shape=[out, out], compiler_params=_params("arbitrary"), name="memory_kv",
    )(mem2d, wk, wv)


def _merge_xattn_kernel(x_ref, att_ref, hg_ref, mk_ref, mv_ref, gfox_ref, wo_ref, ln1g_ref, ln1b_ref,
                        wmq_ref, wmo_ref, ln2g_ref, ln2b_ref, o_ref, *, alpha):
    parts = _row_parts(x_ref.shape[1])
    head = lambda hh: slice(hh * MEM_HEAD_DIM, (hh + 1) * MEM_HEAD_DIM)
    h = [jnp.concatenate([_rms_norm(att_ref[0, r, :], gfox_ref[...]).astype(BF16), hg_ref[0, r, :]], axis=1)
         for r in parts]
    mix = [_dot(hi, wo_ref[...]) for hi in h]
    x1 = [_layer_norm(alpha * x_ref[0, r, :] + mi, ln1g_ref[...], ln1b_ref[...]) for r, mi in zip(parts, mix)]
    qm = [_dot(xi.astype(BF16), wmq_ref[...]) for xi in x1]
    qm = [(qi * (MEM_HEAD_DIM ** -0.5)).astype(BF16) for qi in qm]
    if len(mk_ref.shape) == 4:
        n_rows = N_MEM * N_MEM_HEADS
        mk = mk_ref[0].reshape(n_rows, MEM_HEAD_DIM).astype(BF16)
        mv = mv_ref[0].reshape(n_rows, MEM_HEAD_DIM).astype(BF16)
        own = lambda hh, sh: (lax.broadcasted_iota(jnp.int32, sh.shape, 1) & (N_MEM_HEADS - 1)) == hh
        s = [[_dot_nt(qi[:, head(hh)], mk) for hh in range(N_MEM_HEADS)] for qi in qm]
        s = [[jnp.where(own(hh, sh), sh, -jnp.inf) for hh, sh in enumerate(si)] for si in s]
        values = lambda hh: mv
    else:
        mk = mk_ref[0].astype(BF16)
        mv = mv_ref[0].astype(BF16)
        s = [[_dot_nt(qi[:, head(hh)], mk[:, head(hh)]) for hh in range(N_MEM_HEADS)] for qi in qm]
        values = lambda hh: mv[:, head(hh)]
    p = [[jnp.exp(sh - jnp.max(sh, axis=1, keepdims=True)) for sh in si] for si in s]
    o = [[_dot(ph.astype(BF16), values(hh)) / jnp.sum(ph, axis=1, keepdims=True)
          for hh, ph in enumerate(pi)] for pi in p]
    y = [_dot(jnp.concatenate(oi, axis=1).astype(BF16), wmo_ref[...]) for oi in o]
    for r, xi, yi in zip(parts, x1, y):
        o_ref[0, r, :] = _layer_norm(alpha * xi + yi, ln2g_ref[...], ln2b_ref[...])


def _merge_xattn(x, att, hg, mk, mv, gfox, wo, ln1g, ln1b, wmq, wmo, ln2g, ln2b, *, tm, alpha):
    G, R, D = x.shape
    row = lambda w: pl.BlockSpec((1, tm, w), lambda g, j: (g, j, 0))
    mem = pl.BlockSpec((1,) + mk.shape[1:], lambda g, j: (g,) + (0,) * (mk.ndim - 1))
    full = lambda a: pl.BlockSpec(a.shape, lambda g, j: (0,) * a.ndim)
    return pl.pallas_call(
        functools.partial(_merge_xattn_kernel, alpha=alpha),
        grid=(G, R // tm),
        in_specs=[row(D), row(FOX_WIDTH), row(GMLP_WIDTH), mem, mem] +
                 [full(a) for a in (gfox, wo, ln1g, ln1b, wmq, wmo, ln2g, ln2b)],
        out_specs=row(D), out_shape=jax.ShapeDtypeStruct((G, R, D), F32),
        compiler_params=_params("arbitrary", "arbitrary"), name=f"merge_xattn_{tm}",
    )(x, att, hg, mk, mv, gfox, wo, ln1g, ln1b, wmq, wmo, ln2g, ln2b)


def _ffn_kernel(x_ref, wg_ref, wu_ref, wd_ref, g_ref, b_ref, o_ref, h_ref, *, alpha, d_ff):
    parts = _row_parts(x_ref.shape[1])
    xb = [x_ref[0, r, :].astype(BF16) for r in parts]
    for c0 in range(0, d_ff, FFN_COL_CHUNK):
        cols = slice(c0, min(c0 + FFN_COL_CHUNK, d_ff))
        for r, xi in zip(parts, xb):
            gate = _dot(xi, wg_ref[:, cols])
            up = _dot(xi, wu_ref[:, cols])
            h_ref[r, cols] = (jax.nn.silu(gate) * up).astype(BF16)
    y = [_dot(h_ref[r, :], wd_ref[...]) for r in parts]
    for r, yi in zip(parts, y):
        o_ref[0, r, :] = _layer_norm(alpha * x_ref[0, r, :] + yi, g_ref[...], b_ref[...])


def _ffn(x, wg, wu, wd, g, b, *, tm, alpha):
    G, R, D = x.shape
    d_ff = wg.shape[1]
    assert d_ff % LANES == 0
    row = pl.BlockSpec((1, tm, D), lambda gi, j: (gi, j, 0))
    full = lambda a: pl.BlockSpec(a.shape, lambda gi, j: (0,) * a.ndim, pipeline_mode=pl.Buffered(1))
    return pl.pallas_call(
        functools.partial(_ffn_kernel, alpha=alpha, d_ff=d_ff),
        grid=(G, R // tm),
        in_specs=[row] + [full(a) for a in (wg, wu, wd, g, b)],
        out_specs=row, out_shape=jax.ShapeDtypeStruct((G, R, D), F32),
        scratch_shapes=[pltpu.VMEM((tm, d_ff), BF16)],
        compiler_params=_params("arbitrary", "arbitrary"), name=f"ffn_{tm}",
    )(x, wg, wu, wd, g, b)


def _row(a):
    return a.reshape(1, -1)


def kernel(x_prompt, x_sample, cache_fox_k, cache_fox_v, cache_fox_logf, cache_mem_k, cache_mem_v, mem_prompt, w_in, b_f, g_fox_out, g_gmlp_out, sgu_ln_g, sgu_ln_b, w_s, b_s, w_o, ln1_g, ln1_b, w_mq, w_mk, w_mv, w_mo, ln2_g, ln2_b, w_gate, w_up, w_down, ln3_g, ln3_b):
    depth = w_in.shape[0]
    B, S, D = x_prompt.shape
    Bd, T, _ = x_sample.shape
    P = cache_fox_k.shape[2]
    alpha = (2.0 * depth) ** 0.25
    assert D == D_MODEL and S % FOX_BLOCK == 0 and S % PROMPT_ROW_TILE == 0
    assert PROMPT_ROW_TILE % GMLP_CHUNK == 0 and (T & (T - 1)) == 0 and N_FOX_HEADS * T == LANES
    assert P % SAMPLE_KEY_CHUNK == 0 and S % TAIL_ROW_TILE == 0 and S % FFN_ROW_TILE == 0

    yp, ys = x_prompt, x_sample
    outs = [[] for _ in range(9)]
    for l in range(depth):
        w = w_in[l].astype(BF16)
        wq, wk, wv = (w[:, i * FOX_WIDTH:(i + 1) * FOX_WIDTH] for i in range(3))
        f0 = 3 * FOX_WIDTH
        wf = jnp.pad(w[:, f0:f0 + N_FOX_HEADS], ((0, 0), (0, LANES - N_FOX_HEADS)))
        bfp = jnp.pad(_row(b_f[l]), ((0, 0), (0, LANES - N_FOX_HEADS)))
        wg = w[:, f0 + N_FOX_HEADS:]
        lng, lnb, gout = _row(sgu_ln_g[l]), _row(sgu_ln_b[l]), _row(g_gmlp_out[l])
        tail_w = (_row(g_fox_out[l]), w_o[l].astype(BF16), _row(ln1_g[l]), _row(ln1_b[l]),
                  w_mq[l].astype(BF16), w_mo[l].astype(BF16), _row(ln2_g[l]), _row(ln2_b[l]))
        ffn_w = (w_gate[l].astype(BF16), w_up[l].astype(BF16), w_down[l].astype(BF16),
                 _row(ln3_g[l]), _row(ln3_b[l]))

        def mix_params(n):
            bias = jnp.repeat(b_s[l][:, :n].T, GMLP_GROUP_DIM, axis=1)
            return w_s[l][:, :n, :n], bias

        ws_p, bs_p = mix_params(GMLP_CHUNK)
        k, v, lf, hg, qt, kx, vxt, stats = _in_proj_mix(
            yp, wq, wk, wv, wf, bfp, wg, lng, lnb, ws_p, bs_p, gout,
            tm=PROMPT_ROW_TILE, chunk=GMLP_CHUNK, seg=None, prompt=True)
        att = _fox_prompt(qt, kx, vxt, stats, tb=FOX_BLOCK)
        mk, mv = _memory_kv(mem_prompt.reshape(B * N_MEM, D), w_mk[l].astype(BF16), w_mv[l].astype(BF16))
        mk, mv = mk.reshape(B, N_MEM, D), mv.reshape(B, N_MEM, D)
        x2 = _merge_xattn(yp, att, hg, mk, mv, *tail_w, tm=TAIL_ROW_TILE, alpha=alpha)
        yp = _ffn(x2, *ffn_w, tm=FFN_ROW_TILE, alpha=alpha)
        heads_last = lambda a: a.reshape(B, N_FOX_HEADS, FOX_HEAD_DIM, S).transpose(0, 3, 1, 2)
        outs[0].append(heads_last(k))
        outs[1].append(heads_last(v))
        outs[2].append(lf.transpose(0, 2, 1))
        outs[3].append(mk.reshape(B, N_MEM, N_MEM_HEADS, MEM_HEAD_DIM))
        outs[4].append(mv.reshape(B, N_MEM, N_MEM_HEADS, MEM_HEAD_DIM))

        ws_s, bs_s = mix_params(T)
        k, v, lf, hg, q, ct, gv = _in_proj_mix(
            ys.reshape(1, Bd * T, D), wq, wk, wv, wf, bfp, wg, lng, lnb, ws_s, bs_s, gout,
            tm=Bd * T, chunk=T, seg=T, prompt=False)
        per_b = lambda a: a.reshape(Bd, T, a.shape[-1])
        cn = ct.reshape(N_FOX_HEADS, Bd, T).transpose(1, 0, 2)
        cn = jnp.pad(cn, ((0, 0), (0, 0), (0, LANES - T)))
        dims_first = lambda a: a.transpose(0, 2, 3, 1).reshape(Bd, FOX_WIDTH, P)
        att = _fox_sample(per_b(q), per_b(k), per_b(v), cn,
                          dims_first(cache_fox_k[l]), dims_first(cache_fox_v[l]),
                          jnp.swapaxes(cache_fox_logf[l], 1, 2))
        x2 = _merge_xattn(ys, att, per_b(hg), cache_mem_k[l], cache_mem_v[l], *tail_w, tm=T, alpha=alpha)
        ys = _ffn(x2.reshape(1, Bd * T, D), *ffn_w, tm=Bd * T, alpha=alpha).reshape(Bd, T, D)
        outs[5].append(k.reshape(Bd, T, N_FOX_HEADS, FOX_HEAD_DIM))
        outs[6].append(v.reshape(Bd, T, N_FOX_HEADS, FOX_HEAD_DIM))
        outs[7].append(per_b(lf))
        outs[8].append(per_b(gv))
    return (yp, ys) + tuple(jnp.stack(o) for o in outs)
```
